```python
import math
import jax, jax.numpy as jnp
from jax import lax
import numpy as np

D_MODEL = 2048
BATCH = 4
SEQ = 2048
DEPTH = 1

HG_HEADS = 16
HG_KDIM = 128
HG_VDIM = 128
HG_WIDTH = HG_HEADS * HG_KDIM
HG_CHUNK = 64
ATTN_GROUPS = ((128, 1), (512, 4), (2048, 16))
ATTN_HEADS_PER_GROUP = 4
HEAD_DIM = 128
ATTN_WIDTH = len(ATTN_GROUPS) * ATTN_HEADS_PER_GROUP * HEAD_DIM
ATTN_OUT = ATTN_HEADS_PER_GROUP * HEAD_DIM
ROPE_THETA = 10000.0
D_FF = 5632
CONV_WIDTH = 3
NORM_EPS = 1e-6
NEG_INF = -1e30
IN_COLS = 4 * HG_WIDTH + 3 * ATTN_WIDTH + 2 * D_MODEL

kernel_name = "hybrid_hgrn2_dilated_attn_convffn"


def rmsnorm(x, g):
    x32 = x.astype(jnp.float32)
    y = x32 * lax.rsqrt(jnp.mean(x32 * x32, axis=-1, keepdims=True) + NORM_EPS)
    return (y * g.astype(jnp.float32)).astype(x.dtype)


def rope(x):
    T, Dh = x.shape[1], x.shape[-1]
    half = Dh // 2
    inv_freq = jnp.exp(-math.log(ROPE_THETA) * jnp.arange(half, dtype=jnp.float32) * (2.0 / Dh))
    ang = jnp.arange(T, dtype=jnp.float32)[:, None] * inv_freq[None, :]
    cos = jnp.cos(ang)[None, :, None, :]
    sin = jnp.sin(ang)[None, :, None, :]
    x32 = x.astype(jnp.float32)
    x1, x2 = x32[..., :half], x32[..., half:]
    return jnp.concatenate([x1 * cos - x2 * sin, x2 * cos + x1 * sin], axis=-1).astype(x.dtype)


def hgrn2_branch(q_raw, f_raw, i_raw, g_raw, lb, norm_g):
    B, T, _ = q_raw.shape
    H, K, V, C = HG_HEADS, HG_KDIM, HG_VDIM, HG_CHUNK
    N = T // C
    f32 = jnp.float32
    z = f_raw.astype(f32)
    lb = lb.astype(f32)
    log_f = jnp.logaddexp(jnp.log(lb), jnp.log1p(-lb) + jax.nn.log_sigmoid(z))
    k = (1.0 - lb) * jax.nn.sigmoid(-z)
    q = jax.nn.silu(q_raw.astype(f32))
    v = i_raw.astype(f32)

    def chunks(a, dim):
        return a.reshape(B, N, C, H, dim).transpose(1, 0, 3, 2, 4)

    xs = (chunks(q, K), chunks(k, K), chunks(log_f, K), chunks(v, V))
    causal = jnp.tril(jnp.ones((C, C), dtype=bool))[:, :, None]

    def step(S, inp):
        qc, kc, gc, vc = inp
        G = jnp.cumsum(gc, axis=-2)
        G_last = G[..., -1:, :]
        rel = G[..., :, None, :] - G[..., None, :, :]
        decay = jnp.exp(jnp.where(causal, rel, NEG_INF))
        A = jnp.einsum('bhtk,bhsk,bhtsk->bhts', qc, kc, decay)
        o = (jnp.einsum('bhts,bhsv->bhtv', A, vc)
             + jnp.einsum('bhtk,bhkv->bhtv', qc * jnp.exp(G), S))
        S = (jnp.exp(G_last)[..., 0, :, None] * S
             + jnp.einsum('bhsk,bhsv->bhkv', kc * jnp.exp(G_last - G), vc))
        return S, o

    S0 = jnp.zeros((B, H, K, V), f32)
    _, o = lax.scan(step, S0, xs)
    o = o.transpose(1, 0, 3, 2, 4).reshape(B, T, H, V)
    o = o * lax.rsqrt(jnp.mean(o * o, axis=-1, keepdims=True) + NORM_EPS) * norm_g.astype(f32)
    o = o * jax.nn.silu(g_raw.astype(f32).reshape(B, T, H, V))
    return o.reshape(B, T, H * V).astype(q_raw.dtype)


def banded_causal_attention(q, k, v, steps):
    P = steps
    L, Dh = q.shape[-2], q.shape[-1]
    lead = q.shape[:-2]
    n = -(-L // P)
    Lp = n * P
    nd = len(lead)
    qp = jnp.pad(q, [(0, 0)] * nd + [(0, Lp - L), (0, 0)])
    kp = jnp.pad(k, [(0, 0)] * nd + [(P, Lp - L), (0, 0)])
    vp = jnp.pad(v, [(0, 0)] * nd + [(P, Lp - L), (0, 0)])
    qb = qp.reshape(lead + (n, P, Dh))

    def band(a):
        return jnp.concatenate([a[..., :Lp, :].reshape(lead + (n, P, Dh)),
                                a[..., P:, :].reshape(lead + (n, P, Dh))], axis=-2)

    kb, vb = band(kp), band(vp)
    s = jnp.einsum('...nqd,...nkd->...nqk', qb, kb).astype(jnp.float32) * (Dh ** -0.5)
    q_pos = jnp.arange(n)[:, None] * P + jnp.arange(P)[None, :]
    k_pos = jnp.arange(n)[:, None] * P - P + jnp.arange(2 * P)[None, :]
    rel = q_pos[:, :, None] - k_pos[:, None, :]
    mask = (rel >= 0) & (rel <= steps) & (k_pos[:, None, :] >= 0)
    s = jnp.where(mask, s, NEG_INF)
    lse = jax.nn.logsumexp(s, axis=-1)
    p = jnp.exp(s - lse[..., None])
    o = jnp.einsum('...nqk,...nkd->...nqd', p, vb.astype(jnp.float32))
    o = o.reshape(lead + (Lp, Dh))[..., :L, :]
    lse = lse.reshape(lead + (Lp,))[..., :L]
    return o, lse


def dilated_attention_branch(q_raw, k_raw, v_raw):
    B, T, _ = q_raw.shape
    G, Hg, Dh = len(ATTN_GROUPS), ATTN_HEADS_PER_GROUP, HEAD_DIM
    q = rope(q_raw.reshape(B, T, G * Hg, Dh)).reshape(B, T, G, Hg, Dh)
    k = rope(k_raw.reshape(B, T, G * Hg, Dh)).reshape(B, T, G, Hg, Dh)
    v = v_raw.reshape(B, T, G, Hg, Dh)
    outs, lses = [], []
    for g, (window, dil) in enumerate(ATTN_GROUPS):
        L = T // dil

        def strided(a):
            return a.reshape(B, L, dil, Hg, Dh).transpose(0, 2, 3, 1, 4)

        o, lse = banded_causal_attention(strided(q[:, :, g]), strided(k[:, :, g]),
                                         strided(v[:, :, g]), window // dil)
        outs.append(o.transpose(0, 3, 1, 2, 4).reshape(B, T, Hg, Dh))
        lses.append(lse.transpose(0, 3, 1, 2).reshape(B, T, Hg))
    alpha = jax.nn.softmax(jnp.stack(lses, axis=0), axis=0)
    o = jnp.sum(alpha[..., None] * jnp.stack(outs, axis=0), axis=0)
    return o.reshape(B, T, Hg * Dh).astype(q_raw.dtype)


def conv_ffn(u, w_up, conv_w, conv_b, w_down):
    hup = u @ w_up
    C = hup.shape[-1]
    hc = lax.conv_general_dilated(hup, conv_w[:, None, :], window_strides=(1,),
                                  padding=[(CONV_WIDTH - 1, 0)],
                                  dimension_numbers=('NWC', 'WIO', 'NWC'),
                                  feature_group_count=C) + conv_b
    a, b = hc[..., :D_FF], hc[..., D_FF:]
    return (jax.nn.gelu(a) * b) @ w_down


def setup_inputs(seed: int = 0) -> dict:
    key = jax.random.key(seed)
    ks = jax.random.split(key, 14)
    nrm = jax.random.normal
    f32 = jnp.float32
    return {
        'x': nrm(ks[0], (BATCH, SEQ, D_MODEL), f32),
        'norm1_g': 1.0 + 0.02 * nrm(ks[1], (DEPTH, D_MODEL), f32),
        'w_in': nrm(ks[2], (DEPTH, D_MODEL, IN_COLS), f32) * D_MODEL ** -0.5,
        'hg_lb_param': 0.5 * nrm(ks[3], (DEPTH + 1, HG_WIDTH), f32),
        'hg_norm_g': 1.0 + 0.02 * nrm(ks[4], (DEPTH, HG_VDIM), f32),
        'w_proj_a': nrm(ks[5], (DEPTH, HG_WIDTH, D_MODEL), f32) * HG_WIDTH ** -0.5,
        'w_proj_b': nrm(ks[6], (DEPTH, ATTN_OUT, D_MODEL), f32) * ATTN_OUT ** -0.5,
        'w_out': nrm(ks[7], (DEPTH, D_MODEL, D_MODEL), f32) * D_MODEL ** -0.5,
        'norm2_g': 1.0 + 0.02 * nrm(ks[8], (DEPTH, D_MODEL), f32),
        'w_up': nrm(ks[9], (DEPTH, D_MODEL, 2 * D_FF), f32) * D_MODEL ** -0.5,
        'conv_w': nrm(ks[10], (DEPTH, CONV_WIDTH, 2 * D_FF), f32) * CONV_WIDTH ** -0.5,
        'conv_b': 0.01 * nrm(ks[11], (DEPTH, 2 * D_FF), f32),
        'w_down': nrm(ks[12], (DEPTH, D_FF, D_MODEL), f32) * D_FF ** -0.5,
        'norm_f_g': 1.0 + 0.02 * nrm(ks[13], (D_MODEL,), f32),
    }


def reference(x, norm1_g, w_in, hg_lb_param, hg_norm_g, w_proj_a, w_proj_b, w_out,
              norm2_g, w_up, conv_w, conv_b, w_down, norm_f_g):
    lower_bounds = jnp.cumsum(jax.nn.softmax(hg_lb_param.astype(jnp.float32), axis=0), axis=0)
    sizes = (HG_WIDTH,) * 4 + (ATTN_WIDTH,) * 3 + (D_MODEL,) * 2
    offsets = [int(o) for o in np.cumsum(sizes)[:-1]]
    h = x
    for l in range(DEPTH):
        u = rmsnorm(h, norm1_g[l])
        proj = u @ w_in[l]
        hq, hf, hi, hg, aq, ak, av, gate_a, gate_b = jnp.split(proj, offsets, axis=-1)
        ya = hgrn2_branch(hq, hf, hi, hg, lower_bounds[l], hg_norm_g[l]) @ w_proj_a[l]
        yb = dilated_attention_branch(aq, ak, av) @ w_proj_b[l]
        mix = jax.nn.sigmoid(gate_a) * ya + jax.nn.sigmoid(gate_b) * yb
        h = h + (mix @ w_out[l]).astype(h.dtype)
        v = rmsnorm(h, norm2_g[l])
        h = h + conv_ffn(v, w_up[l], conv_w[l], conv_b[l], w_down[l]).astype(h.dtype)
    return rmsnorm(h, norm_f_g)
```

```python
import functools
import math

import numpy as np
import jax
import jax.numpy as jnp
from jax import lax
from jax.experimental import pallas as pl
from jax.experimental.pallas import tpu as pltpu

D_MODEL = 2048
SEQ = 2048
HG_HEADS = 16
HG_DIM = 128
HG_WIDTH = HG_HEADS * HG_DIM
ATTN_GROUPS = ((128, 1), (512, 4), (2048, 16))
ATTN_HEADS_PER_GROUP = 4
HEAD_DIM = 128
ATTN_WIDTH = len(ATTN_GROUPS) * ATTN_HEADS_PER_GROUP * HEAD_DIM
ATTN_OUT = ATTN_HEADS_PER_GROUP * HEAD_DIM
ROPE_THETA = 10000.0
D_FF = 5632
NORM_EPS = 1e-6
NEG_INF = -1e30
IN_COLS = 4 * HG_WIDTH + 3 * ATTN_WIDTH + 2 * D_MODEL

_OFF_HQ = 0
_OFF_HF = HG_WIDTH // 128
_OFF_HI = 2 * HG_WIDTH // 128
_OFF_HG = 3 * HG_WIDTH // 128
_OFF_AQ = 4 * HG_WIDTH // 128
_OFF_AK = _OFF_AQ + ATTN_WIDTH // 128
_OFF_AV = _OFF_AK + ATTN_WIDTH // 128
_OFF_GA = _OFF_AV + ATTN_WIDTH // 128
_OFF_GB = _OFF_GA + D_MODEL // 128

LANES = 128
BF16_SUBLANES = 16
VMEM_LIMIT = 56 * 1024 * 1024

F32 = jnp.float32
BF16 = jnp.bfloat16


def _dot(a, b):
    return jnp.dot(a, b, preferred_element_type=F32)


def _dot_nt(a, b):
    return lax.dot_general(a, b, (((1,), (1,)), ((), ())), preferred_element_type=F32)


def _dot_tn(a, b):
    return lax.dot_general(a, b, (((0,), (0,)), ((), ())), preferred_element_type=F32)


def _rmsnorm_rows(x, g):
    ms = jnp.mean(x * x, axis=-1, keepdims=True)
    return x * lax.rsqrt(ms + NORM_EPS) * g


def _params(*sem):
    return pltpu.CompilerParams(dimension_semantics=sem, vmem_limit_bytes=VMEM_LIMIT)


def _inproj_kernel(x_ref, g_ref, w_ref, o_ref, u_ref):
    @pl.when(pl.program_id(1) == 0)
    def _():
        u_ref[...] = _rmsnorm_rows(x_ref[...], g_ref[...]).astype(BF16)

    o_ref[...] = _dot(u_ref[...], w_ref[...])


def _in_proj(x2, norm_g, w_bf, tm=1024, tn=768):
    m, d = x2.shape
    n = w_bf.shape[1]
    return pl.pallas_call(
        _inproj_kernel,
        grid=(m // tm, n // tn),
        in_specs=[
            pl.BlockSpec((tm, d), lambda i, j: (i, 0)),
            pl.BlockSpec((1, d), lambda i, j: (0, 0)),
            pl.BlockSpec((d, tn), lambda i, j: (0, j)),
        ],
        out_specs=pl.BlockSpec((tm, tn), lambda i, j: (i, j)),
        out_shape=jax.ShapeDtypeStruct((m, n), F32),
        scratch_shapes=[pltpu.VMEM((tm, d), BF16)],
        compiler_params=_params("parallel", "arbitrary"),
        name="in_proj",
    )(x2, norm_g.reshape(1, d), w_bf)


HG_CHUNK = 128


def _hgrn_tables(c):
    t = np.arange(c)[:, None]
    s = np.arange(c)[None, :]
    x = t ^ s
    lvl = np.where(x > 0, np.floor(np.log2(np.maximum(x, 1))), -1).astype(np.int32)
    lvl = np.where(s > t, -2, lvl).astype(np.int32)
    tri = (s <= t).astype(np.float32)
    return jnp.asarray(tri, BF16), jnp.asarray(lvl, jnp.int32)


def _hgrn_kernel(q_ref, f_ref, i_ref, g_ref, lbp_ref, ng_ref, tri_ref, lvl_ref, o_ref, *, chunk):
    c = chunk
    seq = q_ref.shape[0]
    p = lbp_ref[...]
    pe = jnp.exp(p - jnp.max(p, axis=0, keepdims=True))
    lb = pe[0:1, :] / jnp.sum(pe, axis=0, keepdims=True)
    oml = 1.0 - lb
    ng = ng_ref[...]

    row = lax.broadcasted_iota(jnp.int32, (c, LANES), 0)

    def body(ci, st):
        r0 = pl.multiple_of(ci * c, c)
        rows = pl.ds(r0, c)
        zq = q_ref[rows, :]
        zf = f_ref[rows, :]
        vv = i_ref[rows, :].astype(BF16)
        zg = g_ref[rows, :]

        glog = jnp.log(lb + oml * jax.nn.sigmoid(zf))
        kk = oml * jax.nn.sigmoid(-zf)
        qq = zq * jax.nn.sigmoid(zq)

        g1 = glog.astype(BF16)
        r1 = glog - g1.astype(F32)
        g2 = r1.astype(BF16)
        g3 = (r1 - g2.astype(F32)).astype(BF16)
        gc = _dot(tri_ref[...], jnp.concatenate([g1, g2, g3], axis=1))
        gcum = gc[:, :LANES] + gc[:, LANES:2 * LANES] + gc[:, 2 * LANES:]
        glast = gcum[c - 1:c, :]

        qg = (qq * jnp.exp(gcum)).astype(BF16)
        kd = (kk * jnp.exp(glast - gcum)).astype(BF16)
        o = _dot_nt(qg, st.astype(BF16))
        st_new = st * jnp.exp(glast) + _dot_tn(vv, kd)

        lvl = lvl_ref[...]
        a = jnp.where(lvl == -1, _dot_nt(qq.astype(BF16), kk.astype(BF16)), 0.0)
        h = c // 2
        while h >= 1:
            if h >= 4:
                g3d = gcum.reshape(c // (2 * h), 2 * h, LANES)
                ref = jnp.broadcast_to(g3d[:, h - 1:h, :], g3d.shape).reshape(c, LANES)
                x = -jnp.abs(gcum - ref)
            elif h == 2:
                pos = row & 3
                up = pltpu.roll(glog, c - 1, axis=0)
                dn = pltpu.roll(glog, 1, axis=0)
                x = jnp.where(pos == 0, up, jnp.where(pos == 1, 0.0, jnp.where(pos == 2, glog, glog + dn)))
            else:
                x = jnp.where((row & 1) == 1, glog, 0.0)
            e = jnp.exp(x)
            al = _dot_nt((qq * e).astype(BF16), (kk * e).astype(BF16))
            a = jnp.where(lvl == int(math.log2(h)), al, a)
            h //= 2
        o = o + _dot(a.astype(BF16), vv)

        y = _rmsnorm_rows(o, ng) * (zg * jax.nn.sigmoid(zg))
        o_ref[rows, :] = y.astype(o_ref.dtype)
        return st_new

    lax.fori_loop(0, seq // c, body, jnp.zeros((HG_DIM, HG_DIM), F32))


def _hgrn2(proj, lb_param, norm_g, batch):
    c = HG_CHUNK
    tri, lvl = _hgrn_tables(c)
    m = proj.shape[0]

    def col(off):
        return pl.BlockSpec((SEQ, LANES), lambda b, h, off=off: (b, off + h))

    return pl.pallas_call(
        functools.partial(_hgrn_kernel, chunk=c),
        grid=(batch, HG_HEADS),
        in_specs=[
            col(_OFF_HQ), col(_OFF_HF), col(_OFF_HI), col(_OFF_HG),
            pl.BlockSpec((lb_param.shape[0], LANES), lambda b, h: (0, h)),
            pl.BlockSpec((1, LANES), lambda b, h: (0, 0)),
            pl.BlockSpec((c, c), lambda b, h: (0, 0)),
            pl.BlockSpec((c, c), lambda b, h: (0, 0)),
        ],
        out_specs=pl.BlockSpec((SEQ, LANES), lambda b, h: (b, h)),
        out_shape=jax.ShapeDtypeStruct((m, HG_WIDTH), BF16),
        compiler_params=_params("parallel", "parallel"),
        name="hgrn2",
    )(proj, proj, proj, proj, lb_param, norm_g.reshape(1, LANES), tri, lvl)


ATTN_BLOCK = 128


def _rope_tables():
    half = HEAD_DIM // 2
    inv_freq = jnp.exp(-math.log(ROPE_THETA) * jnp.arange(half, dtype=F32) * (2.0 / HEAD_DIM))
    ang = jnp.arange(SEQ, dtype=F32)[:, None] * inv_freq[None, :]
    cos, sin = jnp.cos(ang), jnp.sin(ang)
    return jnp.concatenate([cos, cos], axis=1), jnp.concatenate([-sin, sin], axis=1)


def _attn_kernel(q0, k0, v0, q1, k1, v1, q2, k2, v2, cos_ref, sin_ref, o_ref, og_ref, lse_ref):
    blk = ATTN_BLOCK
    scale = HEAD_DIM ** -0.5
    qi = lax.broadcasted_iota(jnp.int32, (blk, 2 * blk), 0)
    ki = lax.broadcasted_iota(jnp.int32, (blk, 2 * blk), 1)
    mask2 = (ki >= qi) & (ki <= qi + blk)
    mask1 = (lax.broadcasted_iota(jnp.int32, (blk, blk), 1)
             <= lax.broadcasted_iota(jnp.int32, (blk, blk), 0))

    refs = ((q0, k0, v0), (q1, k1, v1), (q2, k2, v2))
    for g, (window, dil) in enumerate(ATTN_GROUPS):
        assert window // dil == blk
        q_ref, k_ref, v_ref = refs[g]
        length = SEQ // dil
        for r in range(dil):
            def ld(ref):
                return ref[...] if dil == 1 else ref[pl.ds(r, length, stride=dil), :]

            cs, sn = ld(cos_ref), ld(sin_ref)
            q = ld(q_ref)
            k = ld(k_ref)
            q = (q * cs + pltpu.roll(q, HEAD_DIM // 2, axis=1) * sn).astype(BF16)
            k = (k * cs + pltpu.roll(k, HEAD_DIM // 2, axis=1) * sn).astype(BF16)
            v = ld(v_ref).astype(BF16)
            for j in range(length // blk):
                qj = q[j * blk:(j + 1) * blk]
                lo = max(j - 1, 0) * blk
                kw, vw = k[lo:(j + 1) * blk], v[lo:(j + 1) * blk]
                s = _dot_nt(qj, kw) * scale
                s = jnp.where(mask1 if j == 0 else mask2, s, NEG_INF)
                mx = jnp.max(s, axis=-1, keepdims=True)
                p = jnp.exp(s - mx)
                den = jnp.sum(p, axis=-1, keepdims=True)
                o = _dot(p.astype(BF16), vw) / den
                lse = jnp.broadcast_to(mx + jnp.log(den), (blk, HEAD_DIM))
                if dil == 1:
                    rows = pl.ds(j * blk, blk)
                else:
                    rows = pl.ds(j * blk * dil + r, blk, stride=dil)
                og_ref.at[g][rows, :] = o
                lse_ref.at[g][rows, :] = lse

    l0, l1, l2 = lse_ref[0], lse_ref[1], lse_ref[2]
    mx = jnp.maximum(jnp.maximum(l0, l1), l2)
    w0, w1, w2 = jnp.exp(l0 - mx), jnp.exp(l1 - mx), jnp.exp(l2 - mx)
    o = (w0 * og_ref[0] + w1 * og_ref[1] + w2 * og_ref[2]) / (w0 + w1 + w2)
    o_ref[...] = o.astype(o_ref.dtype)


def _attention(proj, batch):
    m = proj.shape[0]
    hg = ATTN_HEADS_PER_GROUP
    cos_t, sin_t = _rope_tables()

    def col(off, g):
        return pl.BlockSpec((SEQ, LANES), lambda b, h, o=off + g * hg: (b, o + h))

    in_specs = []
    for g in range(len(ATTN_GROUPS)):
        in_specs += [col(_OFF_AQ, g), col(_OFF_AK, g), col(_OFF_AV, g)]
    in_specs += [pl.BlockSpec((SEQ, LANES), lambda b, h: (0, 0))] * 2
    return pl.pallas_call(
        _attn_kernel,
        grid=(batch, hg),
        in_specs=in_specs,
        out_specs=pl.BlockSpec((SEQ, LANES), lambda b, h: (b, h)),
        out_shape=jax.ShapeDtypeStruct((m, ATTN_OUT), BF16),
        scratch_shapes=[pltpu.VMEM((len(ATTN_GROUPS), SEQ, LANES), F32)] * 2,
        compiler_params=_params("parallel", "parallel"),
        name="dilated_attn",
    )(*([proj] * 9), cos_t, sin_t)


MIX_GATE_COLS = 512


def _mix_kernel(*refs):
    nc = D_MODEL // MIX_GATE_COLS
    oa_ref, ob_ref = refs[0], refs[1]
    ga_refs, gb_refs = refs[2:2 + nc], refs[2 + nc:2 + 2 * nc]
    x_ref, wpa_ref, wpb_ref, wo_ref, n2_ref, h_ref, v_ref = refs[2 + 2 * nc:]
    oa, ob = oa_ref[...], ob_ref[...]
    parts = []
    for c in range(nc):
        cols = slice(c * MIX_GATE_COLS, (c + 1) * MIX_GATE_COLS)
        ya = _dot(oa, wpa_ref[:, cols])
        yb = _dot(ob, wpb_ref[:, cols])
        mix = jax.nn.sigmoid(ga_refs[c][...]) * ya + jax.nn.sigmoid(gb_refs[c][...]) * yb
        parts.append(mix.astype(BF16))
    h = x_ref[...] + _dot(jnp.concatenate(parts, axis=1), wo_ref[...])
    h_ref[...] = h
    v_ref[...] = _rmsnorm_rows(h, n2_ref[...]).astype(v_ref.dtype)


def _mix(oa, ob, proj, x2, wpa, wpb, wo, norm2_g, tm=256):
    m, d = x2.shape
    gc = MIX_GATE_COLS
    nc = d // gc
    resident = dict(pipeline_mode=pl.Buffered(1))

    def gate(off, c):
        return pl.BlockSpec((tm, gc), lambda i, o=off * LANES // gc + c: (i, o))

    return pl.pallas_call(
        _mix_kernel,
        grid=(m // tm,),
        in_specs=[
            pl.BlockSpec((tm, HG_WIDTH), lambda i: (i, 0)),
            pl.BlockSpec((tm, ATTN_OUT), lambda i: (i, 0)),
            *[gate(_OFF_GA, c) for c in range(nc)],
            *[gate(_OFF_GB, c) for c in range(nc)],
            pl.BlockSpec((tm, d), lambda i: (i, 0)),
            pl.BlockSpec(wpa.shape, lambda i: (0, 0), **resident),
            pl.BlockSpec(wpb.shape, lambda i: (0, 0), **resident),
            pl.BlockSpec(wo.shape, lambda i: (0, 0), **resident),
            pl.BlockSpec((1, d), lambda i: (0, 0)),
        ],
        out_specs=[pl.BlockSpec((tm, d), lambda i: (i, 0)), pl.BlockSpec((tm, d), lambda i: (i, 0))],
        out_shape=[jax.ShapeDtypeStruct((m, d), F32), jax.ShapeDtypeStruct((m, d), BF16)],
        compiler_params=_params("parallel"),
        name="mix_outproj",
    )(oa, ob, *([proj] * (2 * nc)), x2, wpa, wpb, wo, norm2_g.reshape(1, d))


def _shift_rows(h, prev, k):
    r = pltpu.roll(h, k, axis=0)
    p = pltpu.roll(prev, k, axis=0)
    row8 = lax.broadcasted_iota(jnp.int32, prev.shape, 0)
    top = jnp.where(row8 < k, p, r[:8])
    return jnp.concatenate([top, r[8:]], axis=0)


def _causal_conv3(h, prev, w, b):
    return w[2:3] * h + w[1:2] * _shift_rows(h, prev, 1) + w[0:1] * _shift_rows(h, prev, 2) + b


def _gelu_tanh(x):
    return x * (0.5 * (1.0 + jnp.tanh(math.sqrt(2.0 / math.pi) * (x + 0.044715 * (x * x * x)))))


def _ffn_up_kernel(v_ref, halo_ref, wa_ref, wb_ref, cwa_ref, cwb_ref, cba_ref, cbb_ref, o_ref, *, sub):
    tm = v_ref.shape[0]
    starts_sequence = (pl.program_id(0) * tm) % SEQ == 0
    halo = jnp.where(starts_sequence, jnp.zeros_like(halo_ref[...]), halo_ref[...])
    wa, wb = wa_ref[...], wb_ref[...]
    cwa, cwb, cba, cbb = cwa_ref[...], cwb_ref[...], cba_ref[...], cbb_ref[...]
    pa = _dot(halo, wa)[BF16_SUBLANES - 8:]
    pb = _dot(halo, wb)[BF16_SUBLANES - 8:]
    for rb in range(tm // sub):
        vs = v_ref[rb * sub:(rb + 1) * sub, :]
        ha, hb = _dot(vs, wa), _dot(vs, wb)
        ca = _causal_conv3(ha, pa, cwa, cba)
        cb = _causal_conv3(hb, pb, cwb, cbb)
        o_ref[rb * sub:(rb + 1) * sub, :] = (_gelu_tanh(ca) * cb).astype(o_ref.dtype)
        pa, pb = ha[sub - 8:], hb[sub - 8:]


def _ffn_up(v, w_up, conv_w, conv_b, tm=1024, tn=512, sub=256):
    m, d = v.shape
    nj = D_FF // tn
    hb = tm // BF16_SUBLANES
    return pl.pallas_call(
        functools.partial(_ffn_up_kernel, sub=sub),
        grid=(m // tm, nj),
        in_specs=[
            pl.BlockSpec((tm, d), lambda i, j: (i, 0)),
            pl.BlockSpec((BF16_SUBLANES, d), lambda i, j: (jnp.maximum(i * hb - 1, 0), 0)),
            pl.BlockSpec((d, tn), lambda i, j: (0, j)),
            pl.BlockSpec((d, tn), lambda i, j: (0, j + nj)),
            pl.BlockSpec((3, tn), lambda i, j: (0, j)),
            pl.BlockSpec((3, tn), lambda i, j: (0, j + nj)),
            pl.BlockSpec((1, tn), lambda i, j: (0, j)),
            pl.BlockSpec((1, tn), lambda i, j: (0, j + nj)),
        ],
        out_specs=pl.BlockSpec((tm, tn), lambda i, j: (i, j)),
        out_shape=jax.ShapeDtypeStruct((m, D_FF), BF16),
        compiler_params=_params("parallel", "parallel"),
        name="ffn_up_conv_gate",
    )(v, v, w_up, w_up, conv_w, conv_w, conv_b.reshape(1, -1), conv_b.reshape(1, -1))


def _ffn_down_kernel(a_ref, w_ref, h_ref, g_ref, o_ref, acc_ref):
    k = pl.program_id(1)

    @pl.when(k == 0)
    def _():
        acc_ref[...] = jnp.zeros_like(acc_ref)

    acc_ref[...] += _dot(a_ref[...], w_ref[...])

    @pl.when(k == pl.num_programs(1) - 1)
    def _():
        o_ref[...] = _rmsnorm_rows(h_ref[...] + acc_ref[...], g_ref[...])


def _ffn_down(act, w_down, h1, norm_g, tm=512, tk=512):
    m, kdim = act.shape
    d = w_down.shape[1]
    return pl.pallas_call(
        _ffn_down_kernel,
        grid=(m // tm, kdim // tk),
        in_specs=[
            pl.BlockSpec((tm, tk), lambda i, k: (i, k)),
            pl.BlockSpec((tk, d), lambda i, k: (k, 0)),
            pl.BlockSpec((tm, d), lambda i, k: (i, 0)),
            pl.BlockSpec((1, d), lambda i, k: (0, 0)),
        ],
        out_specs=pl.BlockSpec((tm, d), lambda i, k: (i, 0)),
        out_shape=jax.ShapeDtypeStruct((m, d), F32),
        scratch_shapes=[pltpu.VMEM((tm, d), F32)],
        compiler_params=_params("parallel", "arbitrary"),
        name="ffn_down_norm",
    )(act, w_down, h1, norm_g.reshape(1, d))


def kernel(x, norm1_g, w_in, hg_lb_param, hg_norm_g, w_proj_a, w_proj_b, w_out,
           norm2_g, w_up, conv_w, conv_b, w_down, norm_f_g):
    batch, seq, d = x.shape
    depth = w_in.shape[0]
    assert (seq, d, depth) == (SEQ, D_MODEL, 1) and w_in.shape[2] == IN_COLS
    h = x.reshape(batch * seq, d)
    for l in range(depth):
        proj = _in_proj(h, norm1_g[l], w_in[l].astype(BF16))
        oa = _hgrn2(proj, hg_lb_param, hg_norm_g[l], batch)
        ob = _attention(proj, batch)
        h, v = _mix(oa, ob, proj, h, w_proj_a[l].astype(BF16), w_proj_b[l].astype(BF16),
                    w_out[l].astype(BF16), norm2_g[l])
        act = _ffn_up(v, w_up[l].astype(BF16), conv_w[l], conv_b[l])
        out = _ffn_down(act, w_down[l].astype(BF16), h, norm_f_g)
    return out.reshape(batch, seq, d)
```

```python
import functools
import math

import numpy as np
import jax
import jax.numpy as jnp
from jax import lax
from jax.experimental import pallas as pl
from jax.experimental.pallas import tpu as pltpu

D_MODEL = 2048
SEQ = 2048
HG_HEADS = 16
HG_DIM = 128
HG_WIDTH = HG_HEADS * HG_DIM
ATTN_GROUPS = ((128, 1), (512, 4), (2048, 16))
ATTN_HEADS_PER_GROUP = 4
HEAD_DIM = 128
ATTN_WIDTH = len(ATTN_GROUPS) * ATTN_HEADS_PER_GROUP * HEAD_DIM
ATTN_OUT = ATTN_HEADS_PER_GROUP * HEAD_DIM
ROPE_THETA = 10000.0
D_FF = 5632
NORM_EPS = 1e-6
NEG_INF = -1e30
IN_COLS = 4 * HG_WIDTH + 3 * ATTN_WIDTH + 2 * D_MODEL

_OFF_HQ = 0
_OFF_HF = HG_WIDTH // 128
_OFF_HI = 2 * HG_WIDTH // 128
_OFF_HG = 3 * HG_WIDTH // 128
_OFF_AQ = 4 * HG_WIDTH // 128
_OFF_AK = _OFF_AQ + ATTN_WIDTH // 128
_OFF_AV = _OFF_AK + ATTN_WIDTH // 128
_OFF_GA = _OFF_AV + ATTN_WIDTH // 128
_OFF_GB = _OFF_GA + D_MODEL // 128

LANES = 128
BF16_SUBLANES = 16
VMEM_LIMIT = 56 * 1024 * 1024
LOG2E = 1.4426950408889634

F32 = jnp.float32
BF16 = jnp.bfloat16


def _dot(a, b):
    return jnp.dot(a, b, preferred_element_type=F32)


def _dot_nt(a, b):
    return lax.dot_general(a, b, (((1,), (1,)), ((), ())), preferred_element_type=F32)


def _dot_tn(a, b):
    return lax.dot_general(a, b, (((0,), (0,)), ((), ())), preferred_element_type=F32)


def _rmsnorm_rows(x, g):
    ms = jnp.mean(x * x, axis=-1, keepdims=True)
    return x * lax.rsqrt(ms + NORM_EPS) * g


def _sigmoid(x):
    return 1.0 / (1.0 + jnp.exp2(x * (-LOG2E)))


def _params(*sem):
    return pltpu.CompilerParams(dimension_semantics=sem, vmem_limit_bytes=VMEM_LIMIT)


def _norm_kernel(x_ref, g_ref, u_ref):
    u_ref[...] = _rmsnorm_rows(x_ref[...], g_ref[...]).astype(u_ref.dtype)


def _norm_bf16(x2, norm_g, tm=512):
    m, d = x2.shape
    return pl.pallas_call(
        _norm_kernel,
        grid=(m // tm,),
        in_specs=[pl.BlockSpec((tm, d), lambda i: (i, 0)), pl.BlockSpec((1, d), lambda i: (0, 0))],
        out_specs=pl.BlockSpec((tm, d), lambda i: (i, 0)),
        out_shape=jax.ShapeDtypeStruct((m, d), BF16),
        compiler_params=_params("parallel"),
        name="norm1",
    )(x2, norm_g.reshape(1, d))


def _inproj_kernel(u_ref, w_ref, o_ref):
    o_ref[...] = _dot(u_ref[...], w_ref[...].astype(BF16))


def _in_proj(u, w, tm=2048, tn=768):
    m, d = u.shape
    n = w.shape[1]
    return pl.pallas_call(
        _inproj_kernel,
        grid=(m // tm, n // tn),
        in_specs=[
            pl.BlockSpec((tm, d), lambda i, j: (i, 0)),
            pl.BlockSpec((d, tn), lambda i, j: (0, j)),
        ],
        out_specs=pl.BlockSpec((tm, tn), lambda i, j: (i, j)),
        out_shape=jax.ShapeDtypeStruct((m, n), F32),
        compiler_params=_params("parallel", "parallel"),
        name="in_proj",
    )(u, w)


HG_CHUNK = 128
HG_HEADS_PER_STEP = 4
HG_UNROLL = 2


def _hgrn_tables(c):
    t = np.arange(c)[:, None]
    s = np.arange(c)[None, :]
    x = t ^ s
    lvl = np.where(x > 0, np.floor(np.log2(np.maximum(x, 1))), -1).astype(np.int32)
    lvl = np.where(s > t, -2, lvl).astype(np.int32)
    tri = (s <= t).astype(np.float32)
    return jnp.asarray(tri, BF16), jnp.asarray(lvl, jnp.int32)


def _neg_abs(d):
    bits = lax.bitcast_convert_type(d, jnp.uint32) | jnp.uint32(0x80000000)
    return lax.bitcast_convert_type(bits, F32)


def _hgrn_chunk(zq, zf, vv, zg, sts, lb, oml, ng, tri, lvl, row):
    heads = range(len(zq))
    c = zq[0].shape[0]
    f = [lb[i] + oml[i] * _sigmoid(zf[i]) for i in heads]
    kk = [1.0 - f[i] for i in heads]
    qq = [zq[i] * _sigmoid(zq[i]) for i in heads]
    gl = [jnp.log(f[i]) * LOG2E for i in heads]

    gcum = []
    for i in heads:
        g1 = gl[i].astype(BF16)
        g2 = (gl[i] - g1.astype(F32)).astype(BF16)
        gc = _dot(tri, jnp.concatenate([g1, g2], axis=1))
        gcum.append(gc[:, :LANES] + gc[:, LANES:])
    glast = [gcum[i][c - 1:c, :] for i in heads]

    qb = [qq[i].astype(BF16) for i in heads]
    kb = [kk[i].astype(BF16) for i in heads]
    o, sts_new, a = [], [], []
    for i in heads:
        qg = (qq[i] * jnp.exp2(gcum[i])).astype(BF16)
        kd = (kk[i] * jnp.exp2(glast[i] - gcum[i])).astype(BF16)
        o.append(_dot_nt(qg, sts[i].astype(BF16)))
        sts_new.append(sts[i] * jnp.exp2(glast[i]) + _dot_tn(vv[i], kd))
        a.append(jnp.where(lvl == -1, _dot_nt(qb[i], kb[i]), 0.0))

    h = c // 2
    while h >= 1:
        in_level = lvl == int(math.log2(h))
        for i in heads:
            if h >= 4:
                g3d = gcum[i].reshape(c // (2 * h), 2 * h, LANES)
                ref = jnp.broadcast_to(g3d[:, h - 1:h, :], g3d.shape).reshape(c, LANES)
                x = _neg_abs(gcum[i] - ref)
            elif h == 2:
                pos = row & 3
                up = pltpu.roll(gl[i], c - 1, axis=0)
                dn = pltpu.roll(gl[i], 1, axis=0)
                x = jnp.where(pos == 0, up, jnp.where(pos == 1, 0.0, jnp.where(pos == 2, gl[i], gl[i] + dn)))
            else:
                x = jnp.where((row & 1) == 1, gl[i], 0.0)
            e = jnp.exp2(x).astype(BF16)
            a[i] = jnp.where(in_level, _dot_nt(qb[i] * e, kb[i] * e), a[i])
        h //= 2
    o = [o[i] + _dot(a[i].astype(BF16), vv[i]) for i in heads]
    y = [_rmsnorm_rows(o[i], ng) * (zg[i] * _sigmoid(zg[i])) for i in heads]
    return y, sts_new


def _hgrn_kernel(q_ref, f_ref, i_ref, g_ref, lbp_ref, ng_ref, tri_ref, lvl_ref, o_ref, *, chunk, heads):
    c = chunk
    seq = q_ref.shape[0]
    p = lbp_ref[...]
    pe = jnp.exp(p - jnp.max(p, axis=0, keepdims=True))
    lb = pe[0:1, :] / jnp.sum(pe, axis=0, keepdims=True)
    oml = 1.0 - lb
    ng = ng_ref[...]
    row = lax.broadcasted_iota(jnp.int32, (c, LANES), 0)

    lanes = [slice(hh * LANES, (hh + 1) * LANES) for hh in range(heads)]

    def body(ci, sts):
        rows = pl.ds(pl.multiple_of(ci * c, c), c)
        y, new = _hgrn_chunk(
            [q_ref[rows, ln] for ln in lanes], [f_ref[rows, ln] for ln in lanes],
            [i_ref[rows, ln].astype(BF16) for ln in lanes], [g_ref[rows, ln] for ln in lanes],
            sts, [lb[:, ln] for ln in lanes], [oml[:, ln] for ln in lanes],
            ng, tri_ref[...], lvl_ref[...], row)
        for hh, ln in enumerate(lanes):
            o_ref[rows, ln] = y[hh].astype(o_ref.dtype)
        return tuple(new)

    lax.fori_loop(0, seq // c, body, tuple(jnp.zeros((HG_DIM, HG_DIM), F32) for _ in range(heads)),
                  unroll=HG_UNROLL)


def _hgrn2(proj, lb_param, norm_g, batch):
    c, hp = HG_CHUNK, HG_HEADS_PER_STEP
    tri, lvl = _hgrn_tables(c)
    m = proj.shape[0]
    w = hp * LANES

    def col(off):
        return pl.BlockSpec((SEQ, w), lambda b, h, o=off // hp: (b, o + h))

    return pl.pallas_call(
        functools.partial(_hgrn_kernel, chunk=c, heads=hp),
        grid=(batch, HG_HEADS // hp),
        in_specs=[
            col(_OFF_HQ), col(_OFF_HF), col(_OFF_HI), col(_OFF_HG),
            pl.BlockSpec((lb_param.shape[0], w), lambda b, h: (0, h)),
            pl.BlockSpec((1, LANES), lambda b, h: (0, 0)),
            pl.BlockSpec((c, c), lambda b, h: (0, 0)),
            pl.BlockSpec((c, c), lambda b, h: (0, 0)),
        ],
        out_specs=pl.BlockSpec((SEQ, w), lambda b, h: (b, h)),
        out_shape=jax.ShapeDtypeStruct((m, HG_WIDTH), BF16),
        compiler_params=_params("parallel", "parallel"),
        name="hgrn2",
    )(proj, proj, proj, proj, lb_param, norm_g.reshape(1, LANES), tri, lvl)


ATTN_BLOCK = 128


def _rope_tables():
    half = HEAD_DIM // 2
    inv_freq = jnp.exp(-math.log(ROPE_THETA) * jnp.arange(half, dtype=F32) * (2.0 / HEAD_DIM))
    ang = jnp.arange(SEQ, dtype=F32)[:, None] * inv_freq[None, :]
    cos, sin = jnp.cos(ang), jnp.sin(ang)
    return jnp.concatenate([cos, cos], axis=1), jnp.concatenate([-sin, sin], axis=1)


def _attn_kernel(q0, k0, v0, q1, k1, v1, q2, k2, v2, cos_ref, sin_ref, o_ref, og_ref, lse_ref):
    blk = ATTN_BLOCK
    scale = HEAD_DIM ** -0.5
    qi = lax.broadcasted_iota(jnp.int32, (blk, 2 * blk), 0)
    ki = lax.broadcasted_iota(jnp.int32, (blk, 2 * blk), 1)
    mask2 = (ki >= qi) & (ki <= qi + blk)
    mask1 = (lax.broadcasted_iota(jnp.int32, (blk, blk), 1)
             <= lax.broadcasted_iota(jnp.int32, (blk, blk), 0))

    refs = ((q0, k0, v0), (q1, k1, v1), (q2, k2, v2))
    for g, (window, dil) in enumerate(ATTN_GROUPS):
        assert window // dil == blk
        q_ref, k_ref, v_ref = refs[g]
        length = SEQ // dil
        for r in range(dil):
            def ld(ref):
                return ref[...] if dil == 1 else ref[pl.ds(r, length, stride=dil), :]

            cs, sn = ld(cos_ref), ld(sin_ref)
            q = ld(q_ref)
            k = ld(k_ref)
            q = (q * cs + pltpu.roll(q, HEAD_DIM // 2, axis=1) * sn).astype(BF16)
            k = (k * cs + pltpu.roll(k, HEAD_DIM // 2, axis=1) * sn).astype(BF16)
            v = ld(v_ref).astype(BF16)
            for j in range(length // blk):
                qj = q[j * blk:(j + 1) * blk]
                lo = max(j - 1, 0) * blk
                kw, vw = k[lo:(j + 1) * blk], v[lo:(j + 1) * blk]
                s = _dot_nt(qj, kw) * scale
                s = jnp.where(mask1 if j == 0 else mask2, s, NEG_INF)
                mx = jnp.max(s, axis=-1, keepdims=True)
                p = jnp.exp(s - mx)
                den = jnp.sum(p, axis=-1, keepdims=True)
                o = _dot(p.astype(BF16), vw) / den
                lse = jnp.broadcast_to(mx + jnp.log(den), (blk, HEAD_DIM))
                if dil == 1:
                    rows = pl.ds(j * blk, blk)
                else:
                    rows = pl.ds(j * blk * dil + r, blk, stride=dil)
                og_ref.at[g][rows, :] = o
                lse_ref.at[g][rows, :] = lse

    l0, l1, l2 = lse_ref[0], lse_ref[1], lse_ref[2]
    mx = jnp.maximum(jnp.maximum(l0, l1), l2)
    w0, w1, w2 = jnp.exp(l0 - mx), jnp.exp(l1 - mx), jnp.exp(l2 - mx)
    o = (w0 * og_ref[0] + w1 * og_ref[1] + w2 * og_ref[2]) / (w0 + w1 + w2)
    o_ref[...] = o.astype(o_ref.dtype)


def _attention(proj, batch):
    m = proj.shape[0]
    hg = ATTN_HEADS_PER_GROUP
    cos_t, sin_t = _rope_tables()

    def col(off, g):
        return pl.BlockSpec((SEQ, LANES), lambda b, h, o=off + g * hg: (b, o + h))

    in_specs = []
    for g in range(len(ATTN_GROUPS)):
        in_specs += [col(_OFF_AQ, g), col(_OFF_AK, g), col(_OFF_AV, g)]
    in_specs += [pl.BlockSpec((SEQ, LANES), lambda b, h: (0, 0))] * 2
    return pl.pallas_call(
        _attn_kernel,
        grid=(batch, hg),
        in_specs=in_specs,
        out_specs=pl.BlockSpec((SEQ, LANES), lambda b, h: (b, h)),
        out_shape=jax.ShapeDtypeStruct((m, ATTN_OUT), BF16),
        scratch_shapes=[pltpu.VMEM((len(ATTN_GROUPS), SEQ, LANES), F32)] * 2,
        compiler_params=_params("parallel", "parallel"),
        name="dilated_attn",
    )(*([proj] * 9), cos_t, sin_t)


MIX_GATE_COLS = 512


def _mix_kernel(*refs):
    nc = D_MODEL // MIX_GATE_COLS
    oa_ref, ob_ref = refs[0], refs[1]
    ga_refs, gb_refs = refs[2:2 + nc], refs[2 + nc:2 + 2 * nc]
    x_ref, wpa_ref, wpb_ref, wo_ref, n2_ref, h_ref, v_ref = refs[2 + 2 * nc:]
    oa, ob = oa_ref[...], ob_ref[...]
    parts = []
    for c in range(nc):
        cols = slice(c * MIX_GATE_COLS, (c + 1) * MIX_GATE_COLS)
        ya = _dot(oa, wpa_ref[:, cols])
        yb = _dot(ob, wpb_ref[:, cols])
        mix = jax.nn.sigmoid(ga_refs[c][...]) * ya + jax.nn.sigmoid(gb_refs[c][...]) * yb
        parts.append(mix.astype(BF16))
    h = x_ref[...] + _dot(jnp.concatenate(parts, axis=1), wo_ref[...])
    h_ref[...] = h
    v_ref[...] = _rmsnorm_rows(h, n2_ref[...]).astype(v_ref.dtype)


def _mix(oa, ob, proj, x2, wpa, wpb, wo, norm2_g, tm=256):
    m, d = x2.shape
    gc = MIX_GATE_COLS
    nc = d // gc
    resident = dict(pipeline_mode=pl.Buffered(1))

    def gate(off, c):
        return pl.BlockSpec((tm, gc), lambda i, o=off * LANES // gc + c: (i, o))

    return pl.pallas_call(
        _mix_kernel,
        grid=(m // tm,),
        in_specs=[
            pl.BlockSpec((tm, HG_WIDTH), lambda i: (i, 0)),
            pl.BlockSpec((tm, ATTN_OUT), lambda i: (i, 0)),
            *[gate(_OFF_GA, c) for c in range(nc)],
            *[gate(_OFF_GB, c) for c in range(nc)],
            pl.BlockSpec((tm, d), lambda i: (i, 0)),
            pl.BlockSpec(wpa.shape, lambda i: (0, 0), **resident),
            pl.BlockSpec(wpb.shape, lambda i: (0, 0), **resident),
            pl.BlockSpec(wo.shape, lambda i: (0, 0), **resident),
            pl.BlockSpec((1, d), lambda i: (0, 0)),
        ],
        out_specs=[pl.BlockSpec((tm, d), lambda i: (i, 0)), pl.BlockSpec((tm, d), lambda i: (i, 0))],
        out_shape=[jax.ShapeDtypeStruct((m, d), F32), jax.ShapeDtypeStruct((m, d), BF16)],
        compiler_params=_params("parallel"),
        name="mix_outproj",
    )(oa, ob, *([proj] * (2 * nc)), x2, wpa, wpb, wo, norm2_g.reshape(1, d))


def _shift_rows(h, prev, k):
    r = pltpu.roll(h, k, axis=0)
    p = pltpu.roll(prev, k, axis=0)
    row8 = lax.broadcasted_iota(jnp.int32, prev.shape, 0)
    top = jnp.where(row8 < k, p, r[:8])
    return jnp.concatenate([top, r[8:]], axis=0)


def _causal_conv3(h, prev, w, b):
    return w[2:3] * h + w[1:2] * _shift_rows(h, prev, 1) + w[0:1] * _shift_rows(h, prev, 2) + b


def _gelu_tanh(x):
    return x * (0.5 * (1.0 + jnp.tanh(math.sqrt(2.0 / math.pi) * (x + 0.044715 * (x * x * x)))))


def _ffn_up_kernel(v_ref, halo_ref, wa_ref, wb_ref, cwa_ref, cwb_ref, cba_ref, cbb_ref, o_ref,
                   wa_s, wb_s, *, sub):
    tm = v_ref.shape[0]
    starts_sequence = (pl.program_id(0) * tm) % SEQ == 0
    halo = jnp.where(starts_sequence, jnp.zeros_like(halo_ref[...]), halo_ref[...])
    wa_s[...] = wa_ref[...].astype(BF16)
    wb_s[...] = wb_ref[...].astype(BF16)
    cwa, cwb, cba, cbb = cwa_ref[...], cwb_ref[...], cba_ref[...], cbb_ref[...]
    pa = _dot(halo, wa_s[...])[BF16_SUBLANES - 8:]
    pb = _dot(halo, wb_s[...])[BF16_SUBLANES - 8:]
    for rb in range(tm // sub):
        vs = v_ref[rb * sub:(rb + 1) * sub, :]
        ha, hb = _dot(vs, wa_s[...]), _dot(vs, wb_s[...])
        ca = _causal_conv3(ha, pa, cwa, cba)
        cb = _causal_conv3(hb, pb, cwb, cbb)
        o_ref[rb * sub:(rb + 1) * sub, :] = (_gelu_tanh(ca) * cb).astype(o_ref.dtype)
        pa, pb = ha[sub - 8:], hb[sub - 8:]


def _ffn_up(v, w_up, conv_w, conv_b, tm=1024, tn=512, sub=256):
    m, d = v.shape
    nj = D_FF // tn
    hb = tm // BF16_SUBLANES
    return pl.pallas_call(
        functools.partial(_ffn_up_kernel, sub=sub),
        grid=(m // tm, nj),
        in_specs=[
            pl.BlockSpec((tm, d), lambda i, j: (i, 0)),
            pl.BlockSpec((BF16_SUBLANES, d), lambda i, j: (jnp.maximum(i * hb - 1, 0), 0)),
            pl.BlockSpec((d, tn), lambda i, j: (0, j)),
            pl.BlockSpec((d, tn), lambda i, j: (0, j + nj)),
            pl.BlockSpec((3, tn), lambda i, j: (0, j)),
            pl.BlockSpec((3, tn), lambda i, j: (0, j + nj)),
            pl.BlockSpec((1, tn), lambda i, j: (0, j)),
            pl.BlockSpec((1, tn), lambda i, j: (0, j + nj)),
        ],
        out_specs=pl.BlockSpec((tm, tn), lambda i, j: (i, j)),
        out_shape=jax.ShapeDtypeStruct((m, D_FF), BF16),
        scratch_shapes=[pltpu.VMEM((d, tn), BF16)] * 2,
        compiler_params=_params("parallel", "parallel"),
        name="ffn_up_conv_gate",
    )(v, v, w_up, w_up, conv_w, conv_w, conv_b.reshape(1, -1), conv_b.reshape(1, -1))


def _ffn_down_kernel(a_ref, w_ref, h_ref, g_ref, o_ref):
    o_ref[...] = _rmsnorm_rows(h_ref[...] + _dot(a_ref[...], w_ref[...]), g_ref[...])


def _ffn_down(act, w_down, h1, norm_g, tm=256):
    m, kdim = act.shape
    d = w_down.shape[1]
    return pl.pallas_call(
        _ffn_down_kernel,
        grid=(m // tm,),
        in_specs=[
            pl.BlockSpec((tm, kdim), lambda i: (i, 0)),
            pl.BlockSpec((kdim, d), lambda i: (0, 0), pipeline_mode=pl.Buffered(1)),
            pl.BlockSpec((tm, d), lambda i: (i, 0)),
            pl.BlockSpec((1, d), lambda i: (0, 0)),
        ],
        out_specs=pl.BlockSpec((tm, d), lambda i: (i, 0)),
        out_shape=jax.ShapeDtypeStruct((m, d), F32),
        compiler_params=_params("parallel"),
        name="ffn_down_norm",
    )(act, w_down, h1, norm_g.reshape(1, d))


def kernel(x, norm1_g, w_in, hg_lb_param, hg_norm_g, w_proj_a, w_proj_b, w_out,
           norm2_g, w_up, conv_w, conv_b, w_down, norm_f_g):
    batch, seq, d = x.shape
    depth = w_in.shape[0]
    assert (seq, d, depth) == (SEQ, D_MODEL, 1) and w_in.shape[2] == IN_COLS
    h = x.reshape(batch * seq, d)
    for l in range(depth):
        proj = _in_proj(_norm_bf16(h, norm1_g[l]), w_in[l])
        oa = _hgrn2(proj, hg_lb_param, hg_norm_g[l], batch)
        ob = _attention(proj, batch)
        h, v = _mix(oa, ob, proj, h, w_proj_a[l].astype(BF16), w_proj_b[l].astype(BF16),
                    w_out[l].astype(BF16), norm2_g[l])
        act = _ffn_up(v, w_up[l], conv_w[l], conv_b[l])
        out = _ffn_down(act, w_down[l].astype(BF16), h, norm_f_g)
    return out.reshape(batch, seq, d)
```

```python
import functools
import math

import numpy as np
import jax
import jax.numpy as jnp
from jax import lax
from jax.experimental import pallas as pl
from jax.experimental.pallas import tpu as pltpu

D_MODEL = 2048
SEQ = 2048
HG_HEADS = 16
HG_DIM = 128
HG_WIDTH = HG_HEADS * HG_DIM
ATTN_GROUPS = ((128, 1), (512, 4), (2048, 16))
ATTN_HEADS_PER_GROUP = 4
HEAD_DIM = 128
ATTN_WIDTH = len(ATTN_GROUPS) * ATTN_HEADS_PER_GROUP * HEAD_DIM
ATTN_OUT = ATTN_HEADS_PER_GROUP * HEAD_DIM
ROPE_THETA = 10000.0
D_FF = 5632
NORM_EPS = 1e-6
NEG_INF = -1e30
IN_COLS = 4 * HG_WIDTH + 3 * ATTN_WIDTH + 2 * D_MODEL

_OFF_HQ = 0
_OFF_HF = HG_WIDTH // 128
_OFF_HI = 2 * HG_WIDTH // 128
_OFF_HG = 3 * HG_WIDTH // 128
_OFF_AQ = 4 * HG_WIDTH // 128
_OFF_AK = _OFF_AQ + ATTN_WIDTH // 128
_OFF_AV = _OFF_AK + ATTN_WIDTH // 128
_OFF_GA = _OFF_AV + ATTN_WIDTH // 128
_OFF_GB = _OFF_GA + D_MODEL // 128

LANES = 128
BF16_SUBLANES = 16
VMEM_LIMIT = 56 * 1024 * 1024
LOG2E = 1.4426950408889634

F32 = jnp.float32
BF16 = jnp.bfloat16


def _dot(a, b):
    return jnp.dot(a, b, preferred_element_type=F32)


def _dot_nt(a, b):
    return lax.dot_general(a, b, (((1,), (1,)), ((), ())), preferred_element_type=F32)


def _dot_tn(a, b):
    return lax.dot_general(a, b, (((0,), (0,)), ((), ())), preferred_element_type=F32)


def _rmsnorm_rows(x, g):
    ms = jnp.mean(x * x, axis=-1, keepdims=True)
    return x * lax.rsqrt(ms + NORM_EPS) * g


def _sigmoid(x):
    return 1.0 / (1.0 + jnp.exp2(x * (-LOG2E)))


def _params(*sem):
    return pltpu.CompilerParams(dimension_semantics=sem, vmem_limit_bytes=VMEM_LIMIT)


def _norm_kernel(x_ref, g_ref, u_ref):
    u_ref[...] = _rmsnorm_rows(x_ref[...], g_ref[...]).astype(u_ref.dtype)


def _norm_bf16(x2, norm_g, tm=512):
    m, d = x2.shape
    return pl.pallas_call(
        _norm_kernel,
        grid=(m // tm,),
        in_specs=[pl.BlockSpec((tm, d), lambda i: (i, 0)), pl.BlockSpec((1, d), lambda i: (0, 0))],
        out_specs=pl.BlockSpec((tm, d), lambda i: (i, 0)),
        out_shape=jax.ShapeDtypeStruct((m, d), BF16),
        compiler_params=_params("parallel"),
        name="norm1",
    )(x2, norm_g.reshape(1, d))


def _inproj_kernel(u_ref, w_ref, *refs):
    n_side = (len(refs) - 1) // 2
    o_ref = refs[n_side]
    o_ref[...] = _dot(u_ref[...], w_ref[...].astype(BF16))
    for src, dst in zip(refs[:n_side], refs[n_side + 1:]):
        dst[...] = src[...].astype(dst.dtype)


def _in_proj(u, w, side_weights, tm=2048, tn=768):
    m, d = u.shape
    n = w.shape[1]
    nj = n // tn
    steps = (m // tm) * nj
    side_in, side_out, side_shape = [], [], []
    for sw in side_weights:
        rows = -(-sw.shape[0] // steps)
        rows = -(-rows // BF16_SUBLANES) * BF16_SUBLANES
        assert sw.shape[0] % rows == 0
        last = sw.shape[0] // rows - 1
        spec = pl.BlockSpec((rows, sw.shape[1]), lambda i, j, last=last: (jnp.minimum(i * nj + j, last), 0))
        side_in.append(spec)
        side_out.append(spec)
        side_shape.append(jax.ShapeDtypeStruct(sw.shape, BF16))
    outs = pl.pallas_call(
        _inproj_kernel,
        grid=(m // tm, nj),
        in_specs=[
            pl.BlockSpec((tm, d), lambda i, j: (i, 0)),
            pl.BlockSpec((d, tn), lambda i, j: (0, j)),
            *side_in,
        ],
        out_specs=[pl.BlockSpec((tm, tn), lambda i, j: (i, j)), *side_out],
        out_shape=[jax.ShapeDtypeStruct((m, n), F32), *side_shape],
        compiler_params=_params("arbitrary", "arbitrary"),
        name="in_proj",
    )(u, w, *side_weights)
    return outs[0], outs[1:]


HG_CHUNK = 128
HG_HEADS_PER_STEP = 4
HG_CHUNKS_PER_STEP = 2


def _hgrn_tables(c):
    t = np.arange(c)[:, None]
    s = np.arange(c)[None, :]
    x = t ^ s
    lvl = np.where(x > 0, np.floor(np.log2(np.maximum(x, 1))), -1).astype(np.int32)
    lvl = np.where(s > t, -2, lvl).astype(np.int32)
    tri = (s <= t).astype(np.float32)
    return jnp.asarray(tri, BF16), jnp.asarray(lvl, jnp.int32)


def _neg_abs(d):
    bits = lax.bitcast_convert_type(d, jnp.uint32) | jnp.uint32(0x80000000)
    return lax.bitcast_convert_type(bits, F32)


def _hgrn_chunk(zq, zf, vv, zg, sts, lb, oml, ng, tri, lvl, row):
    heads = range(len(zq))
    c = zq[0].shape[0]
    f = [lb[i] + oml[i] * _sigmoid(zf[i]) for i in heads]
    kk = [1.0 - f[i] for i in heads]
    qq = [zq[i] * _sigmoid(zq[i]) for i in heads]
    gl = [jnp.log(f[i]) * LOG2E for i in heads]

    gcum = []
    for i in heads:
        g1 = gl[i].astype(BF16)
        g2 = (gl[i] - g1.astype(F32)).astype(BF16)
        gc = _dot(tri, jnp.concatenate([g1, g2], axis=1))
        gcum.append(gc[:, :LANES] + gc[:, LANES:])
    glast = [gcum[i][c - 1:c, :] for i in heads]

    qb = [qq[i].astype(BF16) for i in heads]
    kb = [kk[i].astype(BF16) for i in heads]
    st_in, a = list(sts), []
    for i in heads:
        kd = (kk[i] * jnp.exp2(glast[i] - gcum[i])).astype(BF16)
        st_in.append(st_in[i] * jnp.exp2(glast[i]) + _dot_tn(vv[i], kd))
        a.append(jnp.where(lvl == -1, _dot_nt(qb[i], kb[i]), 0.0))

    h = c // 2
    while h >= 1:
        in_level = lvl == int(math.log2(h))
        for i in heads:
            if h >= 4:
                g3d = gcum[i].reshape(c // (2 * h), 2 * h, LANES)
                ref = jnp.broadcast_to(g3d[:, h - 1:h, :], g3d.shape).reshape(c, LANES)
                x = _neg_abs(gcum[i] - ref)
            elif h == 2:
                pos = row & 3
                up = pltpu.roll(gl[i], c - 1, axis=0)
                dn = pltpu.roll(gl[i], 1, axis=0)
                x = jnp.where(pos == 0, up, jnp.where(pos == 1, 0.0, jnp.where(pos == 2, gl[i], gl[i] + dn)))
            else:
                x = jnp.where((row & 1) == 1, gl[i], 0.0)
            e = jnp.exp2(x).astype(BF16)
            a[i] = jnp.where(in_level, _dot_nt(qb[i] * e, kb[i] * e), a[i])
        h //= 2
    o = []
    for i in heads:
        qg = (qq[i] * jnp.exp2(gcum[i])).astype(BF16)
        o.append(_dot_nt(qg, st_in[i].astype(BF16)) + _dot(a[i].astype(BF16), vv[i]))
    y = [_rmsnorm_rows(o[i], ng) * (zg[i] * _sigmoid(zg[i])) for i in heads]
    return y, st_in[len(zq):]


def _hgrn_kernel(q_ref, f_ref, i_ref, g_ref, lbp_ref, ng_ref, tri_ref, lvl_ref, o_ref, *, chunk, heads):
    c = chunk
    seq = q_ref.shape[0]
    p = lbp_ref[...]
    pe = jnp.exp(p - jnp.max(p, axis=0, keepdims=True))
    lb = pe[0:1, :] / jnp.sum(pe, axis=0, keepdims=True)
    oml = 1.0 - lb
    ng = ng_ref[...]
    row = lax.broadcasted_iota(jnp.int32, (c, LANES), 0)

    lanes = [slice(hh * LANES, (hh + 1) * LANES) for hh in range(heads)]

    nch = HG_CHUNKS_PER_STEP

    def body(ci, sts):
        ent = [(pl.ds(pl.multiple_of((ci * nch + ch) * c, c), c), ln) for ch in range(nch) for ln in lanes]
        y, new = _hgrn_chunk(
            [q_ref[rows, ln] for rows, ln in ent], [f_ref[rows, ln] for rows, ln in ent],
            [i_ref[rows, ln].astype(BF16) for rows, ln in ent], [g_ref[rows, ln] for rows, ln in ent],
            sts, [lb[:, ln] for _, ln in ent], [oml[:, ln] for _, ln in ent],
            ng, tri_ref[...], lvl_ref[...], row)
        for (rows, ln), yy in zip(ent, y):
            o_ref[rows, ln] = yy.astype(o_ref.dtype)
        return tuple(new)

    lax.fori_loop(0, seq // (c * nch), body, tuple(jnp.zeros((HG_DIM, HG_DIM), F32) for _ in range(heads)))


def _hgrn2(proj, lb_param, norm_g, batch):
    c, hp = HG_CHUNK, HG_HEADS_PER_STEP
    tri, lvl = _hgrn_tables(c)
    m = proj.shape[0]
    w = hp * LANES

    def col(off):
        return pl.BlockSpec((SEQ, w), lambda b, h, o=off // hp: (b, o + h))

    return pl.pallas_call(
        functools.partial(_hgrn_kernel, chunk=c, heads=hp),
        grid=(batch, HG_HEADS // hp),
        in_specs=[
            col(_OFF_HQ), col(_OFF_HF), col(_OFF_HI), col(_OFF_HG),
            pl.BlockSpec((lb_param.shape[0], w), lambda b, h: (0, h)),
            pl.BlockSpec((1, LANES), lambda b, h: (0, 0)),
            pl.BlockSpec((c, c), lambda b, h: (0, 0)),
            pl.BlockSpec((c, c), lambda b, h: (0, 0)),
        ],
        out_specs=pl.BlockSpec((SEQ, w), lambda b, h: (b, h)),
        out_shape=jax.ShapeDtypeStruct((m, HG_WIDTH), BF16),
        compiler_params=_params("parallel", "parallel"),
        name="hgrn2",
    )(proj, proj, proj, proj, lb_param, norm_g.reshape(1, LANES), tri, lvl)


ATTN_BLOCK = 128
ATTN_BATCH = 4


def _rope_tables():
    half = HEAD_DIM // 2
    inv_freq = jnp.exp(-math.log(ROPE_THETA) * jnp.arange(half, dtype=F32) * (2.0 / HEAD_DIM))
    ang = jnp.arange(SEQ, dtype=F32)[:, None] * inv_freq[None, :]
    cos, sin = jnp.cos(ang), jnp.sin(ang)
    return jnp.concatenate([cos, cos], axis=1), jnp.concatenate([-sin, sin], axis=1)


def _attn_kernel(q0, k0, v0, q1, k1, v1, q2, k2, v2, cos_ref, sin_ref, o_ref, og_ref, lse_ref):
    blk = ATTN_BLOCK
    scale = HEAD_DIM ** -0.5
    qi = lax.broadcasted_iota(jnp.int32, (blk, 2 * blk), 0)
    ki = lax.broadcasted_iota(jnp.int32, (blk, 2 * blk), 1)
    mask2 = (ki >= qi) & (ki <= qi + blk)
    mask1 = (lax.broadcasted_iota(jnp.int32, (blk, blk), 1)
             <= lax.broadcasted_iota(jnp.int32, (blk, blk), 0))

    def run_batch(items):
        s = [_dot_nt(qj, kw) * (scale * LOG2E) for qj, kw, _, _, _ in items]
        ps, mxs = [], []
        for (_, _, _, mask, _), sj in zip(items, s):
            sj = jnp.where(mask, sj, NEG_INF)
            mx = jnp.max(sj, axis=-1, keepdims=True)
            ps.append(jnp.exp2(sj - mx).astype(BF16))
            mxs.append(mx)
        os = [_dot(p, jnp.concatenate([vw, jnp.ones_like(vw)], axis=1)) for p, (_, _, vw, _, _) in zip(ps, items)]
        for (_, _, _, _, (g, rows)), o, mx in zip(items, os, mxs):
            den = o[:, HEAD_DIM:]
            og_ref.at[g][rows, :] = o[:, :HEAD_DIM] * (1.0 / den)
            lse_ref.at[g][rows, :] = mx + jnp.log(den) * LOG2E

    refs = ((q0, k0, v0), (q1, k1, v1), (q2, k2, v2))
    pending = []
    for g, (window, dil) in enumerate(ATTN_GROUPS):
        assert window // dil == blk
        q_ref, k_ref, v_ref = refs[g]
        length = SEQ // dil
        for r in range(dil):
            def ld(ref):
                return ref[...] if dil == 1 else ref[pl.ds(r, length, stride=dil), :]

            cs, sn = ld(cos_ref), ld(sin_ref)
            q = ld(q_ref)
            k = ld(k_ref)
            q = (q * cs + pltpu.roll(q, HEAD_DIM // 2, axis=1) * sn).astype(BF16)
            k = (k * cs + pltpu.roll(k, HEAD_DIM // 2, axis=1) * sn).astype(BF16)
            v = ld(v_ref).astype(BF16)
            for j in range(length // blk):
                lo = max(j - 1, 0) * blk
                rows = pl.ds(j * blk, blk) if dil == 1 else pl.ds(j * blk * dil + r, blk, stride=dil)
                pending.append((q[j * blk:(j + 1) * blk], k[lo:(j + 1) * blk], v[lo:(j + 1) * blk],
                                mask1 if j == 0 else mask2, (g, rows)))
                if len(pending) == ATTN_BATCH:
                    run_batch(pending)
                    pending = []
    assert not pending

    l0, l1, l2 = lse_ref[0], lse_ref[1], lse_ref[2]
    mx = jnp.maximum(jnp.maximum(l0, l1), l2)
    w0, w1, w2 = jnp.exp2(l0 - mx), jnp.exp2(l1 - mx), jnp.exp2(l2 - mx)
    o = (w0 * og_ref[0] + w1 * og_ref[1] + w2 * og_ref[2]) / (w0 + w1 + w2)
    o_ref[...] = o.astype(o_ref.dtype)


def _attention(proj, batch):
    m = proj.shape[0]
    hg = ATTN_HEADS_PER_GROUP
    cos_t, sin_t = _rope_tables()

    def col(off, g):
        return pl.BlockSpec((SEQ, LANES), lambda b, h, o=off + g * hg: (b, o + h))

    in_specs = []
    for g in range(len(ATTN_GROUPS)):
        in_specs += [col(_OFF_AQ, g), col(_OFF_AK, g), col(_OFF_AV, g)]
    in_specs += [pl.BlockSpec((SEQ, LANES), lambda b, h: (0, 0))] * 2
    return pl.pallas_call(
        _attn_kernel,
        grid=(batch, hg),
        in_specs=in_specs,
        out_specs=pl.BlockSpec((SEQ, LANES), lambda b, h: (b, h)),
        out_shape=jax.ShapeDtypeStruct((m, ATTN_OUT), BF16),
        scratch_shapes=[pltpu.VMEM((len(ATTN_GROUPS), SEQ, LANES), F32)] * 2,
        compiler_params=_params("parallel", "parallel"),
        name="dilated_attn",
    )(*([proj] * 9), cos_t, sin_t)


MIX_GATE_COLS = 512


def _mix_kernel(*refs):
    nc = D_MODEL // MIX_GATE_COLS
    oa_ref, ob_ref = refs[0], refs[1]
    ga_refs, gb_refs = refs[2:2 + nc], refs[2 + nc:2 + 2 * nc]
    x_ref, wpa_ref, wpb_ref, wo_ref, n2_ref, h_ref, v_ref = refs[2 + 2 * nc:]
    oa, ob = oa_ref[...], ob_ref[...]
    parts = []
    for c in range(nc):
        cols = slice(c * MIX_GATE_COLS, (c + 1) * MIX_GATE_COLS)
        ya = _dot(oa, wpa_ref[:, cols])
        yb = _dot(ob, wpb_ref[:, cols])
        mix = jax.nn.sigmoid(ga_refs[c][...]) * ya + jax.nn.sigmoid(gb_refs[c][...]) * yb
        parts.append(mix.astype(BF16))
    h = x_ref[...] + _dot(jnp.concatenate(parts, axis=1), wo_ref[...])
    h_ref[...] = h
    v_ref[...] = _rmsnorm_rows(h, n2_ref[...]).astype(v_ref.dtype)


def _mix(oa, ob, proj, x2, wpa, wpb, wo, norm2_g, tm=256):
    m, d = x2.shape
    gc = MIX_GATE_COLS
    nc = d // gc
    resident = dict(pipeline_mode=pl.Buffered(1))

    def gate(off, c):
        return pl.BlockSpec((tm, gc), lambda i, o=off * LANES // gc + c: (i, o))

    return pl.pallas_call(
        _mix_kernel,
        grid=(m // tm,),
        in_specs=[
            pl.BlockSpec((tm, HG_WIDTH), lambda i: (i, 0)),
            pl.BlockSpec((tm, ATTN_OUT), lambda i: (i, 0)),
            *[gate(_OFF_GA, c) for c in range(nc)],
            *[gate(_OFF_GB, c) for c in range(nc)],
            pl.BlockSpec((tm, d), lambda i: (i, 0)),
            pl.BlockSpec(wpa.shape, lambda i: (0, 0), **resident),
            pl.BlockSpec(wpb.shape, lambda i: (0, 0), **resident),
            pl.BlockSpec(wo.shape, lambda i: (0, 0), **resident),
            pl.BlockSpec((1, d), lambda i: (0, 0)),
        ],
        out_specs=[pl.BlockSpec((tm, d), lambda i: (i, 0)), pl.BlockSpec((tm, d), lambda i: (i, 0))],
        out_shape=[jax.ShapeDtypeStruct((m, d), F32), jax.ShapeDtypeStruct((m, d), BF16)],
        compiler_params=_params("parallel"),
        name="mix_outproj",
    )(oa, ob, *([proj] * (2 * nc)), x2, wpa, wpb, wo, norm2_g.reshape(1, d))


def _shift_rows(h, prev, k):
    r = pltpu.roll(h, k, axis=0)
    p = pltpu.roll(prev, k, axis=0)
    row8 = lax.broadcasted_iota(jnp.int32, prev.shape, 0)
    top = jnp.where(row8 < k, p, r[:8])
    return jnp.concatenate([top, r[8:]], axis=0)


def _causal_conv3(h, prev, w, b):
    return w[2:3] * h + w[1:2] * _shift_rows(h, prev, 1) + w[0:1] * _shift_rows(h, prev, 2) + b


def _gelu_tanh_gate(x, y):
    c = math.sqrt(2.0 / math.pi)
    w = x * ((-2.0 * c * LOG2E) + (-2.0 * c * 0.044715 * LOG2E) * (x * x))
    return (x * y) * (1.0 / (1.0 + jnp.exp2(w)))


def _ffn_up_kernel(v_ref, halo_ref, wa_ref, wb_ref, cwa_ref, cwb_ref, cba_ref, cbb_ref, o_ref, *, sub):
    tm = v_ref.shape[0]
    hr = halo_ref.shape[0]
    starts_sequence = (pl.program_id(0) * tm) % SEQ == 0
    halo = jnp.where(starts_sequence, jnp.zeros_like(halo_ref[...]), halo_ref[...])
    wa, wb = wa_ref[...].astype(BF16), wb_ref[...].astype(BF16)
    cwa, cwb, cba, cbb = cwa_ref[...], cwb_ref[...], cba_ref[...], cbb_ref[...]
    pa = pb = None
    for r0 in range(0, tm, sub):
        vs = v_ref[r0:r0 + sub, :]
        if r0 == 0:
            vs = jnp.concatenate([halo, vs], axis=0)
        ha, hb = _dot(vs, wa), _dot(vs, wb)
        if r0 == 0:
            pa, pb = ha[hr - 8:hr], hb[hr - 8:hr]
            ha, hb = ha[hr:], hb[hr:]
        ca = _causal_conv3(ha, pa, cwa, cba)
        cb = _causal_conv3(hb, pb, cwb, cbb)
        o_ref[r0:r0 + sub, :] = _gelu_tanh_gate(ca, cb).astype(o_ref.dtype)
        pa, pb = ha[sub - 8:], hb[sub - 8:]


def _ffn_up(v, w_up, conv_w, conv_b, tm=2048, tn=512, sub=256):
    m, d = v.shape
    nj = D_FF // tn
    hb = tm // BF16_SUBLANES
    return pl.pallas_call(
        functools.partial(_ffn_up_kernel, sub=sub),
        grid=(m // tm, nj),
        in_specs=[
            pl.BlockSpec((tm, d), lambda i, j: (i, 0)),
            pl.BlockSpec((BF16_SUBLANES, d), lambda i, j: (jnp.maximum(i * hb - 1, 0), 0)),
            pl.BlockSpec((d, tn), lambda i, j: (0, j)),
            pl.BlockSpec((d, tn), lambda i, j: (0, j + nj)),
            pl.BlockSpec((3, tn), lambda i, j: (0, j)),
            pl.BlockSpec((3, tn), lambda i, j: (0, j + nj)),
            pl.BlockSpec((1, tn), lambda i, j: (0, j)),
            pl.BlockSpec((1, tn), lambda i, j: (0, j + nj)),
        ],
        out_specs=pl.BlockSpec((tm, tn), lambda i, j: (i, j)),
        out_shape=jax.ShapeDtypeStruct((m, D_FF), BF16),
        compiler_params=_params("parallel", "parallel"),
        name="ffn_up_conv_gate",
    )(v, v, w_up, w_up, conv_w, conv_w, conv_b.reshape(1, -1), conv_b.reshape(1, -1))


def _ffn_down_kernel(a_ref, w_ref, h_ref, g_ref, o_ref):
    o_ref[...] = _rmsnorm_rows(h_ref[...] + _dot(a_ref[...], w_ref[...]), g_ref[...])


def _ffn_down(act, w_down, h1, norm_g, tm=256):
    m, kdim = act.shape
    d = w_down.shape[1]
    return pl.pallas_call(
        _ffn_down_kernel,
        grid=(m // tm,),
        in_specs=[
            pl.BlockSpec((tm, kdim), lambda i: (i, 0)),
            pl.BlockSpec((kdim, d), lambda i: (0, 0), pipeline_mode=pl.Buffered(1)),
            pl.BlockSpec((tm, d), lambda i: (i, 0)),
            pl.BlockSpec((1, d), lambda i: (0, 0)),
        ],
        out_specs=pl.BlockSpec((tm, d), lambda i: (i, 0)),
        out_shape=jax.ShapeDtypeStruct((m, d), F32),
        compiler_params=_params("parallel"),
        name="ffn_down_norm",
    )(act, w_down, h1, norm_g.reshape(1, d))


def kernel(x, norm1_g, w_in, hg_lb_param, hg_norm_g, w_proj_a, w_proj_b, w_out,
           norm2_g, w_up, conv_w, conv_b, w_down, norm_f_g):
    batch, seq, d = x.shape
    depth = w_in.shape[0]
    assert (seq, d, depth) == (SEQ, D_MODEL, 1) and w_in.shape[2] == IN_COLS
    h = x.reshape(batch * seq, d)
    for l in range(depth):
        proj, (wpa, wpb, wo, wdn) = _in_proj(_norm_bf16(h, norm1_g[l]), w_in[l],
                                              (w_proj_a[l], w_proj_b[l], w_out[l], w_down[l]))
        oa = _hgrn2(proj, hg_lb_param, hg_norm_g[l], batch)
        ob = _attention(proj, batch)
        h, v = _mix(oa, ob, proj, h, wpa, wpb, wo, norm2_g[l])
        act = _ffn_up(v, w_up[l], conv_w[l], conv_b[l])
        out = _ffn_down(act, wdn, h, norm_f_g)
    return out.reshape(batch, seq, d)
```

```python
import functools
import math

import numpy as np
import jax
import jax.numpy as jnp
from jax import lax
from jax.experimental import pallas as pl
from jax.experimental.pallas import tpu as pltpu

D_MODEL = 2048
SEQ = 2048
HG_HEADS = 16
HG_DIM = 128
HG_WIDTH = HG_HEADS * HG_DIM
ATTN_GROUPS = ((128, 1), (512, 4), (2048, 16))
ATTN_HEADS_PER_GROUP = 4
HEAD_DIM = 128
ATTN_WIDTH = len(ATTN_GROUPS) * ATTN_HEADS_PER_GROUP * HEAD_DIM
ATTN_OUT = ATTN_HEADS_PER_GROUP * HEAD_DIM
ROPE_THETA = 10000.0
D_FF = 5632
NORM_EPS = 1e-6
NEG_INF = -1e30
IN_COLS = 4 * HG_WIDTH + 3 * ATTN_WIDTH + 2 * D_MODEL

_OFF_HQ = 0
_OFF_HF = HG_WIDTH // 128
_OFF_HI = 2 * HG_WIDTH // 128
_OFF_HG = 3 * HG_WIDTH // 128
_OFF_AQ = 4 * HG_WIDTH // 128
_OFF_AK = _OFF_AQ + ATTN_WIDTH // 128
_OFF_AV = _OFF_AK + ATTN_WIDTH // 128
_OFF_GA = 4 * HG_WIDTH // 128
_OFF_GB = _OFF_GA + D_MODEL // 128

LANES = 128
BF16_SUBLANES = 16
VMEM_LIMIT = 56 * 1024 * 1024
LOG2E = 1.4426950408889634

F32 = jnp.float32
BF16 = jnp.bfloat16


def _dot(a, b):
    return jnp.dot(a, b, preferred_element_type=F32)


def _dot_nt(a, b):
    return lax.dot_general(a, b, (((1,), (1,)), ((), ())), preferred_element_type=F32)


def _dot_tn(a, b):
    return lax.dot_general(a, b, (((0,), (0,)), ((), ())), preferred_element_type=F32)


def _rmsnorm_rows(x, g):
    ms = jnp.mean(x * x, axis=-1, keepdims=True)
    return x * lax.rsqrt(ms + NORM_EPS) * g


def _sigmoid(x):
    return 1.0 / (1.0 + jnp.exp2(x * (-LOG2E)))


def _params(*sem):
    return pltpu.CompilerParams(dimension_semantics=sem, vmem_limit_bytes=VMEM_LIMIT)


def _norm_kernel(x_ref, g_ref, u_ref):
    u_ref[...] = _rmsnorm_rows(x_ref[...], g_ref[...]).astype(u_ref.dtype)


def _norm_bf16(x2, norm_g, tm=512):
    m, d = x2.shape
    return pl.pallas_call(
        _norm_kernel,
        grid=(m // tm,),
        in_specs=[pl.BlockSpec((tm, d), lambda i: (i, 0)), pl.BlockSpec((1, d), lambda i: (0, 0))],
        out_specs=pl.BlockSpec((tm, d), lambda i: (i, 0)),
        out_shape=jax.ShapeDtypeStruct((m, d), BF16),
        compiler_params=_params("parallel"),
        name="norm1",
    )(x2, norm_g.reshape(1, d))


def _inproj_kernel(u_ref, w_ref, *refs):
    n_side = (len(refs) - 1) // 2
    o_ref = refs[n_side]
    o_ref[...] = _dot(u_ref[...], w_ref[...].astype(BF16))
    for src, dst in zip(refs[:n_side], refs[n_side + 1:]):
        dst[...] = src[...].astype(dst.dtype)


def _in_proj(u, w, side_weights, tm=2048, tn=512):
    m, d = u.shape
    n_hg, skip = 4 * HG_WIDTH // tn, 3 * ATTN_WIDTH // tn
    nj = n_hg + 2 * D_MODEL // tn
    steps = (m // tm) * nj
    side_in, side_out, side_shape = [], [], []
    for sw in side_weights:
        rows = -(-sw.shape[0] // steps)
        rows = -(-rows // BF16_SUBLANES) * BF16_SUBLANES
        assert sw.shape[0] % rows == 0
        last = sw.shape[0] // rows - 1
        spec = pl.BlockSpec((rows, sw.shape[1]), lambda i, j, last=last: (jnp.minimum(i * nj + j, last), 0))
        side_in.append(spec)
        side_out.append(spec)
        side_shape.append(jax.ShapeDtypeStruct(sw.shape, BF16))
    outs = pl.pallas_call(
        _inproj_kernel,
        grid=(m // tm, nj),
        in_specs=[
            pl.BlockSpec((tm, d), lambda i, j: (i, 0)),
            pl.BlockSpec((d, tn), lambda i, j: (0, jnp.where(j < n_hg, j, j + skip))),
            *side_in,
        ],
        out_specs=[pl.BlockSpec((tm, tn), lambda i, j: (i, j)), *side_out],
        out_shape=[jax.ShapeDtypeStruct((m, nj * tn), F32), *side_shape],
        compiler_params=_params("arbitrary", "arbitrary"),
        name="in_proj",
    )(u, w, *side_weights)
    return outs[0], outs[1:]


def _attn_proj_kernel(u_ref, w_ref, cos_ref, sin_ref, o_ref, scr, scr2, *, dil, sub):
    w = w_ref[...].astype(BF16)
    tm = u_ref.shape[0]
    length, n = tm // dil, sub // dil
    for r0 in range(0, tm, sub):
        acc = _dot(u_ref[r0:r0 + sub, :], w)
        cs, sn = cos_ref[r0:r0 + sub, :], sin_ref[r0:r0 + sub, :]
        for hh in range(acc.shape[1] // HEAD_DIM):
            ln = slice(hh * HEAD_DIM, (hh + 1) * HEAD_DIM)
            x = acc[:, ln]
            y = x * cs + pltpu.roll(x, HEAD_DIM // 2, axis=1) * sn
            if dil == 1:
                o_ref[r0:r0 + sub, ln] = y.astype(o_ref.dtype)
                continue
            scr[hh] = y
            for r1 in range(4):
                part = scr[hh, pl.ds(r1, sub // 4, stride=4), :]
                if dil == 4:
                    dst = r1 * length + r0 // dil
                    o_ref[dst:dst + n, ln] = part.astype(o_ref.dtype)
                    continue
                scr2[hh, r1] = part
                for r2 in range(4):
                    dst = _attn_slot(r1 + 4 * r2) * length + r0 // dil
                    o_ref[dst:dst + n, ln] = scr2[hh, r1, pl.ds(r2, n, stride=4), :].astype(o_ref.dtype)


def _attn_slot(r):
    return (r % 4) * 4 + r // 4


def _attn_proj(u, w, group, tm=SEQ, sub=512):
    m, d = u.shape
    dil = ATTN_GROUPS[group][1]
    assert dil in (1, 4, 16)
    tn = ATTN_HEADS_PER_GROUP * HEAD_DIM
    half = HEAD_DIM // 2
    inv_freq = jnp.exp(-math.log(ROPE_THETA) * jnp.arange(half, dtype=F32) * (2.0 / HEAD_DIM))
    ang = jnp.arange(SEQ, dtype=F32)[:, None] * inv_freq[None, :]
    cos, sin = jnp.cos(ang), jnp.sin(ang)
    one, zero = jnp.ones((SEQ, HEAD_DIM), F32), jnp.zeros((SEQ, HEAD_DIM), F32)
    cos_t = jnp.stack([jnp.concatenate([cos, cos], axis=1), one])
    sin_t = jnp.stack([jnp.concatenate([-sin, sin], axis=1), zero])
    first = (_OFF_AQ * LANES) // tn + group
    per = ATTN_WIDTH // tn
    table = pl.BlockSpec((None, SEQ, HEAD_DIM), lambda i, j: (j // 2, 0, 0))
    return pl.pallas_call(
        functools.partial(_attn_proj_kernel, dil=dil, sub=sub),
        grid=(m // tm, 3),
        in_specs=[
            pl.BlockSpec((tm, d), lambda i, j: (i, 0)),
            pl.BlockSpec((d, tn), lambda i, j: (0, first + per * j)),
            table, table,
        ],
        out_specs=pl.BlockSpec((tm, tn), lambda i, j: (i, j)),
        out_shape=jax.ShapeDtypeStruct((m, 3 * tn), BF16),
        scratch_shapes=[pltpu.VMEM((tn // HEAD_DIM, sub, HEAD_DIM), F32),
                        pltpu.VMEM((tn // HEAD_DIM, 4, sub // 4, HEAD_DIM), F32)],
        compiler_params=_params("parallel", "parallel"),
        name="attn_proj",
    )(u, w, cos_t, sin_t)


HG_CHUNK = 128
HG_HEADS_PER_STEP = 4
HG_CHUNKS_PER_STEP = 2


def _hgrn_tables(c):
    t = np.arange(c)[:, None]
    s = np.arange(c)[None, :]
    x = t ^ s
    lvl = np.where(x > 0, np.floor(np.log2(np.maximum(x, 1))), -1).astype(np.int32)
    lvl = np.where(s > t, -2, lvl).astype(np.int32)
    tri = (s <= t).astype(np.float32)
    return jnp.asarray(tri, BF16), jnp.asarray(lvl, jnp.int32)


def _neg_abs(d):
    bits = lax.bitcast_convert_type(d, jnp.uint32) | jnp.uint32(0x80000000)
    return lax.bitcast_convert_type(bits, F32)


def _hgrn_chunk(zq, zf, vv, zg, sts, lb, oml, ng, tri, lvl, row):
    heads = range(len(zq))
    c = zq[0].shape[0]
    f = [lb[i] + oml[i] * _sigmoid(zf[i]) for i in heads]
    kk = [1.0 - f[i] for i in heads]
    qq = [zq[i] * _sigmoid(zq[i]) for i in heads]
    gl = [jnp.log(f[i]) * LOG2E for i in heads]

    gcum = []
    for i in heads:
        g1 = gl[i].astype(BF16)
        g2 = (gl[i] - g1.astype(F32)).astype(BF16)
        gc = _dot(tri, jnp.concatenate([g1, g2], axis=1))
        gcum.append(gc[:, :LANES] + gc[:, LANES:])
    glast = [gcum[i][c - 1:c, :] for i in heads]

    qb = [qq[i].astype(BF16) for i in heads]
    kb = [kk[i].astype(BF16) for i in heads]
    st_in, a = list(sts), []
    for i in heads:
        kd = (kk[i] * jnp.exp2(glast[i] - gcum[i])).astype(BF16)
        st_in.append(st_in[i] * jnp.exp2(glast[i]) + _dot_tn(vv[i], kd))
        a.append(jnp.where(lvl == -1, _dot_nt(qb[i], kb[i]), 0.0))

    h = c // 2
    while h >= 1:
        in_level = lvl == int(math.log2(h))
        for i in heads:
            if h >= 4:
                g3d = gcum[i].reshape(c // (2 * h), 2 * h, LANES)
                ref = jnp.broadcast_to(g3d[:, h - 1:h, :], g3d.shape).reshape(c, LANES)
                x = _neg_abs(gcum[i] - ref)
            elif h == 2:
                pos = row & 3
                up = pltpu.roll(gl[i], c - 1, axis=0)
                dn = pltpu.roll(gl[i], 1, axis=0)
                x = jnp.where(pos == 0, up, jnp.where(pos == 1, 0.0, jnp.where(pos == 2, gl[i], gl[i] + dn)))
            else:
                x = jnp.where((row & 1) == 1, gl[i], 0.0)
            e = jnp.exp2(x).astype(BF16)
            a[i] = jnp.where(in_level, _dot_nt(qb[i] * e, kb[i] * e), a[i])
        h //= 2
    o = []
    for i in heads:
        qg = (qq[i] * jnp.exp2(gcum[i])).astype(BF16)
        o.append(_dot_nt(qg, st_in[i].astype(BF16)) + _dot(a[i].astype(BF16), vv[i]))
    y = [_rmsnorm_rows(o[i], ng) * (zg[i] * _sigmoid(zg[i])) for i in heads]
    return y, st_in[len(zq):]


def _hgrn_kernel(q_ref, f_ref, i_ref, g_ref, lbp_ref, ng_ref, tri_ref, lvl_ref, o_ref, *, chunk, heads):
    c = chunk
    seq = q_ref.shape[0]
    p = lbp_ref[...]
    pe = jnp.exp(p - jnp.max(p, axis=0, keepdims=True))
    lb = pe[0:1, :] / jnp.sum(pe, axis=0, keepdims=True)
    oml = 1.0 - lb
    ng = ng_ref[...]
    row = lax.broadcasted_iota(jnp.int32, (c, LANES), 0)

    lanes = [slice(hh * LANES, (hh + 1) * LANES) for hh in range(heads)]

    nch = HG_CHUNKS_PER_STEP

    def body(ci, sts):
        ent = [(pl.ds(pl.multiple_of((ci * nch + ch) * c, c), c), ln) for ch in range(nch) for ln in lanes]
        y, new = _hgrn_chunk(
            [q_ref[rows, ln] for rows, ln in ent], [f_ref[rows, ln] for rows, ln in ent],
            [i_ref[rows, ln].astype(BF16) for rows, ln in ent], [g_ref[rows, ln] for rows, ln in ent],
            sts, [lb[:, ln] for _, ln in ent], [oml[:, ln] for _, ln in ent],
            ng, tri_ref[...], lvl_ref[...], row)
        for (rows, ln), yy in zip(ent, y):
            o_ref[rows, ln] = yy.astype(o_ref.dtype)
        return tuple(new)

    lax.fori_loop(0, seq // (c * nch), body, tuple(jnp.zeros((HG_DIM, HG_DIM), F32) for _ in range(heads)))


def _hgrn2(proj, lb_param, norm_g, batch):
    c, hp = HG_CHUNK, HG_HEADS_PER_STEP
    tri, lvl = _hgrn_tables(c)
    m = proj.shape[0]
    w = hp * LANES

    def col(off):
        return pl.BlockSpec((SEQ, w), lambda b, h, o=off // hp: (b, o + h))

    return pl.pallas_call(
        functools.partial(_hgrn_kernel, chunk=c, heads=hp),
        grid=(batch, HG_HEADS // hp),
        in_specs=[
            col(_OFF_HQ), col(_OFF_HF), col(_OFF_HI), col(_OFF_HG),
            pl.BlockSpec((lb_param.shape[0], w), lambda b, h: (0, h)),
            pl.BlockSpec((1, LANES), lambda b, h: (0, 0)),
            pl.BlockSpec((c, c), lambda b, h: (0, 0)),
            pl.BlockSpec((c, c), lambda b, h: (0, 0)),
        ],
        out_specs=pl.BlockSpec((SEQ, w), lambda b, h: (b, h)),
        out_shape=jax.ShapeDtypeStruct((m, HG_WIDTH), BF16),
        compiler_params=_params("parallel", "parallel"),
        name="hgrn2",
    )(proj, proj, proj, proj, lb_param, norm_g.reshape(1, LANES), tri, lvl)


ATTN_BLOCK = 128
ATTN_BATCH = 8


def _attn_kernel(q0, k0, v0, q1, k1, v1, q2, k2, v2, o_ref, og_ref, lse_ref):
    blk = ATTN_BLOCK
    scale = HEAD_DIM ** -0.5
    qi = lax.broadcasted_iota(jnp.int32, (blk, 2 * blk), 0)
    ki = lax.broadcasted_iota(jnp.int32, (blk, 2 * blk), 1)
    mask2 = (ki >= qi) & (ki <= qi + blk)
    mask1 = (lax.broadcasted_iota(jnp.int32, (blk, blk), 1)
             <= lax.broadcasted_iota(jnp.int32, (blk, blk), 0))

    def run_batch(items):
        s = [_dot_nt(qj, kw) * (scale * LOG2E) for qj, kw, _, _, _ in items]
        ps, mxs = [], []
        for (_, _, _, mask, _), sj in zip(items, s):
            sj = jnp.where(mask, sj, NEG_INF)
            mx = jnp.max(sj, axis=-1, keepdims=True)
            ps.append(jnp.exp2(sj - mx).astype(BF16))
            mxs.append(mx)
        os = [_dot(p, jnp.concatenate([vw, jnp.ones_like(vw)], axis=1)) for p, (_, _, vw, _, _) in zip(ps, items)]
        for (_, _, _, _, (g, rows)), o, mx in zip(items, os, mxs):
            den = o[:, HEAD_DIM:]
            og_ref.at[g][rows, :] = o[:, :HEAD_DIM] * (1.0 / den)
            lse_ref.at[g][rows, :] = mx + jnp.log(den) * LOG2E

    refs = ((q0, k0, v0), (q1, k1, v1), (q2, k2, v2))
    pending = []
    for g, (window, dil) in enumerate(ATTN_GROUPS):
        assert window // dil == blk
        q_ref, k_ref, v_ref = refs[g]
        length = SEQ // dil
        for r in range(dil):
            base = (_attn_slot(r) if dil == 16 else r) * length
            for j in range(length // blk):
                lo, hi = base + max(j - 1, 0) * blk, base + (j + 1) * blk
                rows = pl.ds(j * blk, blk) if dil == 1 else pl.ds(j * blk * dil + r, blk, stride=dil)
                pending.append((q_ref[hi - blk:hi, :], k_ref[lo:hi, :], v_ref[lo:hi, :],
                                mask1 if j == 0 else mask2, (g, rows)))
                if len(pending) == ATTN_BATCH:
                    run_batch(pending)
                    pending = []
    assert not pending

    l0, l1, l2 = lse_ref[0], lse_ref[1], lse_ref[2]
    mx = jnp.maximum(jnp.maximum(l0, l1), l2)
    w0, w1, w2 = jnp.exp2(l0 - mx), jnp.exp2(l1 - mx), jnp.exp2(l2 - mx)
    o = (w0 * og_ref[0] + w1 * og_ref[1] + w2 * og_ref[2]) / (w0 + w1 + w2)
    o_ref[...] = o.astype(o_ref.dtype)


def _attention(qkv, batch):
    m = qkv[0].shape[0]
    hg = ATTN_HEADS_PER_GROUP

    def col(part):
        return pl.BlockSpec((SEQ, LANES), lambda b, h, o=part * hg: (b, o + h))

    return pl.pallas_call(
        _attn_kernel,
        grid=(batch, hg),
        in_specs=[col(0), col(1), col(2)] * len(ATTN_GROUPS),
        out_specs=pl.BlockSpec((SEQ, LANES), lambda b, h: (b, h)),
        out_shape=jax.ShapeDtypeStruct((m, ATTN_OUT), BF16),
        scratch_shapes=[pltpu.VMEM((len(ATTN_GROUPS), SEQ, LANES), F32)] * 2,
        compiler_params=_params("parallel", "parallel"),
        name="dilated_attn",
    )(*[a for g in qkv for a in (g, g, g)])


MIX_GATE_COLS = 512


def _mix_kernel(*refs):
    nc = D_MODEL // MIX_GATE_COLS
    oa_ref, ob_ref = refs[0], refs[1]
    ga_refs, gb_refs = refs[2:2 + nc], refs[2 + nc:2 + 2 * nc]
    x_ref, wpa_ref, wpb_ref, wo_ref, n2_ref, h_ref, v_ref = refs[2 + 2 * nc:]
    oa, ob = oa_ref[...], ob_ref[...]
    parts = []
    for c in range(nc):
        cols = slice(c * MIX_GATE_COLS, (c + 1) * MIX_GATE_COLS)
        ya = _dot(oa, wpa_ref[:, cols])
        yb = _dot(ob, wpb_ref[:, cols])
        mix = jax.nn.sigmoid(ga_refs[c][...]) * ya + jax.nn.sigmoid(gb_refs[c][...]) * yb
        parts.append(mix.astype(BF16))
    h = x_ref[...] + _dot(jnp.concatenate(parts, axis=1), wo_ref[...])
    h_ref[...] = h
    v_ref[...] = _rmsnorm_rows(h, n2_ref[...]).astype(v_ref.dtype)


def _mix(oa, ob, proj, x2, wpa, wpb, wo, norm2_g, tm=256):
    m, d = x2.shape
    gc = MIX_GATE_COLS
    nc = d // gc
    resident = dict(pipeline_mode=pl.Buffered(1))

    def gate(off, c):
        return pl.BlockSpec((tm, gc), lambda i, o=off * LANES // gc + c: (i, o))

    return pl.pallas_call(
        _mix_kernel,
        grid=(m // tm,),
        in_specs=[
            pl.BlockSpec((tm, HG_WIDTH), lambda i: (i, 0)),
            pl.BlockSpec((tm, ATTN_OUT), lambda i: (i, 0)),
            *[gate(_OFF_GA, c) for c in range(nc)],
            *[gate(_OFF_GB, c) for c in range(nc)],
            pl.BlockSpec((tm, d), lambda i: (i, 0)),
            pl.BlockSpec(wpa.shape, lambda i: (0, 0), **resident),
            pl.BlockSpec(wpb.shape, lambda i: (0, 0), **resident),
            pl.BlockSpec(wo.shape, lambda i: (0, 0), **resident),
            pl.BlockSpec((1, d), lambda i: (0, 0)),
        ],
        out_specs=[pl.BlockSpec((tm, d), lambda i: (i, 0)), pl.BlockSpec((tm, d), lambda i: (i, 0))],
        out_shape=[jax.ShapeDtypeStruct((m, d), F32), jax.ShapeDtypeStruct((m, d), BF16)],
        compiler_params=_params("parallel"),
        name="mix_outproj",
    )(oa, ob, *([proj] * (2 * nc)), x2, wpa, wpb, wo, norm2_g.reshape(1, d))


def _shift_rows(h, prev, k):
    r = pltpu.roll(h, k, axis=0)
    p = pltpu.roll(prev, k, axis=0)
    row8 = lax.broadcasted_iota(jnp.int32, prev.shape, 0)
    top = jnp.where(row8 < k, p, r[:8])
    return jnp.concatenate([top, r[8:]], axis=0)


def _causal_conv3(h, prev, w, b):
    return w[2:3] * h + w[1:2] * _shift_rows(h, prev, 1) + w[0:1] * _shift_rows(h, prev, 2) + b


def _gelu_tanh_gate(x, y):
    c = math.sqrt(2.0 / math.pi)
    w = x * ((-2.0 * c * LOG2E) + (-2.0 * c * 0.044715 * LOG2E) * (x * x))
    return (x * y) * (1.0 / (1.0 + jnp.exp2(w)))


def _ffn_up_kernel(v_ref, halo_ref, wa_ref, wb_ref, cwa_ref, cwb_ref, cba_ref, cbb_ref, o_ref, *, sub):
    tm = v_ref.shape[0]
    hr = halo_ref.shape[0]
    starts_sequence = (pl.program_id(0) * tm) % SEQ == 0
    halo = jnp.where(starts_sequence, jnp.zeros_like(halo_ref[...]), halo_ref[...])
    wa, wb = wa_ref[...].astype(BF16), wb_ref[...].astype(BF16)
    cwa, cwb, cba, cbb = cwa_ref[...], cwb_ref[...], cba_ref[...], cbb_ref[...]
    pa = pb = None
    for r0 in range(0, tm, sub):
        vs = v_ref[r0:r0 + sub, :]
        if r0 == 0:
            vs = jnp.concatenate([halo, vs], axis=0)
        ha, hb = _dot(vs, wa), _dot(vs, wb)
        if r0 == 0:
            pa, pb = ha[hr - 8:hr], hb[hr - 8:hr]
            ha, hb = ha[hr:], hb[hr:]
        ca = _causal_conv3(ha, pa, cwa, cba)
        cb = _causal_conv3(hb, pb, cwb, cbb)
        o_ref[r0:r0 + sub, :] = _gelu_tanh_gate(ca, cb).astype(o_ref.dtype)
        pa, pb = ha[sub - 8:], hb[sub - 8:]


def _ffn_up(v, w_up, conv_w, conv_b, tm=2048, tn=512, sub=256):
    m, d = v.shape
    nj = D_FF // tn
    hb = tm // BF16_SUBLANES
    return pl.pallas_call(
        functools.partial(_ffn_up_kernel, sub=sub),
        grid=(m // tm, nj),
        in_specs=[
            pl.BlockSpec((tm, d), lambda i, j: (i, 0)),
            pl.BlockSpec((BF16_SUBLANES, d), lambda i, j: (jnp.maximum(i * hb - 1, 0), 0)),
            pl.BlockSpec((d, tn), lambda i, j: (0, j)),
            pl.BlockSpec((d, tn), lambda i, j: (0, j + nj)),
            pl.BlockSpec((3, tn), lambda i, j: (0, j)),
            pl.BlockSpec((3, tn), lambda i, j: (0, j + nj)),
            pl.BlockSpec((1, tn), lambda i, j: (0, j)),
            pl.BlockSpec((1, tn), lambda i, j: (0, j + nj)),
        ],
        out_specs=pl.BlockSpec((tm, tn), lambda i, j: (i, j)),
        out_shape=jax.ShapeDtypeStruct((m, D_FF), BF16),
        compiler_params=_params("parallel", "parallel"),
        name="ffn_up_conv_gate",
    )(v, v, w_up, w_up, conv_w, conv_w, conv_b.reshape(1, -1), conv_b.reshape(1, -1))


def _ffn_down_kernel(a_ref, w_ref, h_ref, g_ref, o_ref):
    o_ref[...] = _rmsnorm_rows(h_ref[...] + _dot(a_ref[...], w_ref[...]), g_ref[...])


def _ffn_down(act, w_down, h1, norm_g, tm=256):
    m, kdim = act.shape
    d = w_down.shape[1]
    return pl.pallas_call(
        _ffn_down_kernel,
        grid=(m // tm,),
        in_specs=[
            pl.BlockSpec((tm, kdim), lambda i: (i, 0)),
            pl.BlockSpec((kdim, d), lambda i: (0, 0), pipeline_mode=pl.Buffered(1)),
            pl.BlockSpec((tm, d), lambda i: (i, 0)),
            pl.BlockSpec((1, d), lambda i: (0, 0)),
        ],
        out_specs=pl.BlockSpec((tm, d), lambda i: (i, 0)),
        out_shape=jax.ShapeDtypeStruct((m, d), F32),
        compiler_params=_params("parallel"),
        name="ffn_down_norm",
    )(act, w_down, h1, norm_g.reshape(1, d))


def kernel(x, norm1_g, w_in, hg_lb_param, hg_norm_g, w_proj_a, w_proj_b, w_out,
           norm2_g, w_up, conv_w, conv_b, w_down, norm_f_g):
    batch, seq, d = x.shape
    depth = w_in.shape[0]
    assert (seq, d, depth) == (SEQ, D_MODEL, 1) and w_in.shape[2] == IN_COLS
    h = x.reshape(batch * seq, d)
    for l in range(depth):
        u = _norm_bf16(h, norm1_g[l])
        proj, (wpa, wpb, wo, wdn) = _in_proj(u, w_in[l], (w_proj_a[l], w_proj_b[l], w_out[l], w_down[l]))
        oa = _hgrn2(proj, hg_lb_param, hg_norm_g[l], batch)
        ob = _attention([_attn_proj(u, w_in[l], g) for g in range(len(ATTN_GROUPS))], batch)
        h, v = _mix(oa, ob, proj, h, wpa, wpb, wo, norm2_g[l])
        act = _ffn_up(v, w_up[l], conv_w[l], conv_b[l])
        out = _ffn_down(act, wdn, h, norm_f_g)
    return out.reshape(batch, seq, d)
```

```python
import functools
import math

import numpy as np
import jax
import jax.numpy as jnp
from jax import lax
from jax.experimental import pallas as pl
from jax.experimental.pallas import tpu as pltpu

D_MODEL = 2048
SEQ = 2048
HG_HEADS = 16
HG_DIM = 128
HG_WIDTH = HG_HEADS * HG_DIM
ATTN_GROUPS = ((128, 1), (512, 4), (2048, 16))
ATTN_HEADS_PER_GROUP = 4
HEAD_DIM = 128
ATTN_WIDTH = len(ATTN_GROUPS) * ATTN_HEADS_PER_GROUP * HEAD_DIM
ATTN_OUT = ATTN_HEADS_PER_GROUP * HEAD_DIM
ROPE_THETA = 10000.0
D_FF = 5632
NORM_EPS = 1e-6
NEG_INF = -1e30
IN_COLS = 4 * HG_WIDTH + 3 * ATTN_WIDTH + 2 * D_MODEL

_OFF_HQ = 0
_OFF_HF = HG_WIDTH // 128
_OFF_HI = 2 * HG_WIDTH // 128
_OFF_HG = 3 * HG_WIDTH // 128
_OFF_AQ = 4 * HG_WIDTH // 128
_OFF_AK = _OFF_AQ + ATTN_WIDTH // 128
_OFF_AV = _OFF_AK + ATTN_WIDTH // 128
_OFF_GA = 4 * HG_WIDTH // 128
_OFF_GB = _OFF_GA + D_MODEL // 128

LANES = 128
BF16_SUBLANES = 16
VMEM_LIMIT = 56 * 1024 * 1024
LOG2E = 1.4426950408889634

F32 = jnp.float32
BF16 = jnp.bfloat16


def _dot(a, b):
    return jnp.dot(a, b, preferred_element_type=F32)


def _dot_nt(a, b):
    return lax.dot_general(a, b, (((1,), (1,)), ((), ())), preferred_element_type=F32)


def _dot_tn(a, b):
    return lax.dot_general(a, b, (((0,), (0,)), ((), ())), preferred_element_type=F32)


def _rmsnorm_rows(x, g):
    ms = jnp.mean(x * x, axis=-1, keepdims=True)
    return x * lax.rsqrt(ms + NORM_EPS) * g


def _sigmoid(x):
    return 1.0 / (1.0 + jnp.exp2(x * (-LOG2E)))


def _params(*sem):
    return pltpu.CompilerParams(dimension_semantics=sem, vmem_limit_bytes=VMEM_LIMIT)


def _norm_kernel(x_ref, g_ref, u_ref):
    u_ref[...] = _rmsnorm_rows(x_ref[...], g_ref[...]).astype(u_ref.dtype)


def _norm_bf16(x2, norm_g, tm=512):
    m, d = x2.shape
    return pl.pallas_call(
        _norm_kernel,
        grid=(m // tm,),
        in_specs=[pl.BlockSpec((tm, d), lambda i: (i, 0)), pl.BlockSpec((1, d), lambda i: (0, 0))],
        out_specs=pl.BlockSpec((tm, d), lambda i: (i, 0)),
        out_shape=jax.ShapeDtypeStruct((m, d), BF16),
        compiler_params=_params("parallel"),
        name="norm1",
    )(x2, norm_g.reshape(1, d))


def _attn_tile(u_ref, w_ref, cos_ref, sin_ref, o_ref, scr, scr2, *, dil, sub):
    w = w_ref[...].astype(BF16)
    tm = u_ref.shape[0]
    length, n = tm // dil, sub // dil
    for r0 in range(0, tm, sub):
        acc = _dot(u_ref[r0:r0 + sub, :], w)
        cs, sn = cos_ref[r0:r0 + sub, :], sin_ref[r0:r0 + sub, :]
        for hh in range(acc.shape[1] // HEAD_DIM):
            ln = slice(hh * HEAD_DIM, (hh + 1) * HEAD_DIM)
            x = acc[:, ln]
            y = x * cs + pltpu.roll(x, HEAD_DIM // 2, axis=1) * sn
            if dil == 1:
                o_ref[r0:r0 + sub, ln] = y.astype(o_ref.dtype)
                continue
            scr[hh] = y
            for r1 in range(4):
                part = scr[hh, pl.ds(r1, sub // 4, stride=4), :]
                if dil == 4:
                    dst = r1 * length + r0 // dil
                    o_ref[dst:dst + n, ln] = part.astype(o_ref.dtype)
                    continue
                scr2[hh, r1] = part
                for r2 in range(4):
                    dst = _attn_slot(r1 + 4 * r2) * length + r0 // dil
                    o_ref[dst:dst + n, ln] = scr2[hh, r1, pl.ds(r2, n, stride=4), :].astype(o_ref.dtype)


def _attn_slot(r):
    return (r % 4) * 4 + r // 4


def _inproj_kernel(u_ref, w_ref, cos_ref, sin_ref, *refs, n_side, attn_lo, sub):
    side_in, o_main = refs[:n_side], refs[n_side]
    qkv = refs[n_side + 1:n_side + 1 + len(ATTN_GROUPS)]
    side_out = refs[n_side + 1 + len(ATTN_GROUPS):-2]
    scr, scr2 = refs[-2:]
    ng = len(ATTN_GROUPS)
    ja = pl.program_id(1) - attn_lo
    is_attn = (ja >= 0) & (ja < 3 * ng)

    @pl.when(jnp.logical_not(is_attn))
    def _():
        o_main[...] = _dot(u_ref[...], w_ref[...].astype(BF16))

    for g, (_, dil) in enumerate(ATTN_GROUPS):
        @pl.when(is_attn & (lax.rem(ja, ng) == g))
        def _():
            _attn_tile(u_ref, w_ref, cos_ref, sin_ref, qkv[g], scr, scr2, dil=dil, sub=sub)

    for src, dst in zip(side_in, side_out):
        dst[...] = src[...].astype(dst.dtype)


def _in_proj(u, w, side_weights, tm=SEQ, tn=512, sub=512):
    m, d = u.shape
    ng = len(ATTN_GROUPS)
    assert tn == ATTN_HEADS_PER_GROUP * HEAD_DIM and all(dl in (1, 4, 16) for _, dl in ATTN_GROUPS)
    attn_lo = 4 * HG_WIDTH // tn
    n_attn = 3 * ng
    nj = w.shape[1] // tn
    n_main = nj - n_attn
    steps = (m // tm) * nj

    half = HEAD_DIM // 2
    inv_freq = jnp.exp(-math.log(ROPE_THETA) * jnp.arange(half, dtype=F32) * (2.0 / HEAD_DIM))
    ang = jnp.arange(SEQ, dtype=F32)[:, None] * inv_freq[None, :]
    cos, sin = jnp.cos(ang), jnp.sin(ang)
    one, zero = jnp.ones((SEQ, HEAD_DIM), F32), jnp.zeros((SEQ, HEAD_DIM), F32)
    cos_t = jnp.stack([jnp.concatenate([cos, cos], axis=1), one])
    sin_t = jnp.stack([jnp.concatenate([-sin, sin], axis=1), zero])

    def part(j):
        return jnp.clip((j - attn_lo) // ng, 0, 2)

    table = pl.BlockSpec((None, SEQ, HEAD_DIM), lambda i, j: (part(j) // 2, 0, 0))

    def main_col(j):
        return jnp.where(j < attn_lo, j, jnp.maximum(j - n_attn, attn_lo - 1))

    def qkv_spec(g):
        return pl.BlockSpec((tm, tn), lambda i, j: (i, jnp.clip((j - attn_lo - g) // ng, 0, 2)))

    side_specs, side_shape = [], []
    for sw in side_weights:
        rows = -(-sw.shape[0] // steps)
        rows = -(-rows // BF16_SUBLANES) * BF16_SUBLANES
        while sw.shape[0] % rows:
            rows += BF16_SUBLANES
        last = sw.shape[0] // rows - 1
        side_specs.append(pl.BlockSpec((rows, sw.shape[1]), lambda i, j, last=last: (jnp.minimum(i * nj + j, last), 0)))
        side_shape.append(jax.ShapeDtypeStruct(sw.shape, BF16))
    heads = tn // HEAD_DIM
    outs = pl.pallas_call(
        functools.partial(_inproj_kernel, n_side=len(side_weights), attn_lo=attn_lo, sub=sub),
        grid=(m // tm, nj),
        in_specs=[
            pl.BlockSpec((tm, d), lambda i, j: (i, 0), pipeline_mode=pl.Buffered(1)),
            pl.BlockSpec((d, tn), lambda i, j: (0, j)),
            table, table,
            *side_specs,
        ],
        out_specs=[pl.BlockSpec((tm, tn), lambda i, j: (i, main_col(j))),
                   *[qkv_spec(g) for g in range(ng)], *side_specs],
        out_shape=[jax.ShapeDtypeStruct((m, n_main * tn), F32),
                   *[jax.ShapeDtypeStruct((m, 3 * tn), BF16)] * ng, *side_shape],
        scratch_shapes=[pltpu.VMEM((heads, sub, HEAD_DIM), F32), pltpu.VMEM((heads, 4, sub // 4, HEAD_DIM), F32)],
        compiler_params=_params("arbitrary", "arbitrary"),
        name="in_proj",
    )(u, w, cos_t, sin_t, *side_weights)
    return outs[0], outs[1:1 + ng], outs[1 + ng:]


HG_CHUNK = 128
HG_HEADS_PER_STEP = 4
HG_CHUNKS_PER_STEP = 2


def _hgrn_tables(c):
    t = np.arange(c)[:, None]
    s = np.arange(c)[None, :]
    x = t ^ s
    lvl = np.where(x > 0, np.floor(np.log2(np.maximum(x, 1))), -1).astype(np.int32)
    lvl = np.where(s > t, -2, lvl).astype(np.int32)
    tri = (s <= t).astype(np.float32)
    return jnp.asarray(tri, BF16), jnp.asarray(lvl, jnp.int32)


def _neg_abs(d):
    bits = lax.bitcast_convert_type(d, jnp.uint32) | jnp.uint32(0x80000000)
    return lax.bitcast_convert_type(bits, F32)


def _hgrn_chunk(zq, zf, vv, zg, sts, lb, oml, ng, tri, lvl, row):
    heads = range(len(zq))
    c = zq[0].shape[0]
    f = [lb[i] + oml[i] * _sigmoid(zf[i]) for i in heads]
    kk = [1.0 - f[i] for i in heads]
    qq = [zq[i] * _sigmoid(zq[i]) for i in heads]
    gl = [jnp.log(f[i]) * LOG2E for i in heads]

    gcum = []
    for i in heads:
        g1 = gl[i].astype(BF16)
        g2 = (gl[i] - g1.astype(F32)).astype(BF16)
        gc = _dot(tri, jnp.concatenate([g1, g2], axis=1))
        gcum.append(gc[:, :LANES] + gc[:, LANES:])
    glast = [gcum[i][c - 1:c, :] for i in heads]

    qb = [qq[i].astype(BF16) for i in heads]
    kb = [kk[i].astype(BF16) for i in heads]
    st_in, a = list(sts), []
    for i in heads:
        kd = (kk[i] * jnp.exp2(glast[i] - gcum[i])).astype(BF16)
        st_in.append(st_in[i] * jnp.exp2(glast[i]) + _dot_tn(vv[i], kd))
        a.append(jnp.where(lvl == -1, _dot_nt(qb[i], kb[i]), 0.0))

    h = c // 2
    while h >= 1:
        in_level = lvl == int(math.log2(h))
        for i in heads:
            if h >= 4:
                g3d = gcum[i].reshape(c // (2 * h), 2 * h, LANES)
                ref = jnp.broadcast_to(g3d[:, h - 1:h, :], g3d.shape).reshape(c, LANES)
                x = _neg_abs(gcum[i] - ref)
            elif h == 2:
                pos = row & 3
                up = pltpu.roll(gl[i], c - 1, axis=0)
                dn = pltpu.roll(gl[i], 1, axis=0)
                x = jnp.where(pos == 0, up, jnp.where(pos == 1, 0.0, jnp.where(pos == 2, gl[i], gl[i] + dn)))
            else:
                x = jnp.where((row & 1) == 1, gl[i], 0.0)
            e = jnp.exp2(x).astype(BF16)
            a[i] = jnp.where(in_level, _dot_nt(qb[i] * e, kb[i] * e), a[i])
        h //= 2
    o = []
    for i in heads:
        qg = (qq[i] * jnp.exp2(gcum[i])).astype(BF16)
        o.append(_dot_nt(qg, st_in[i].astype(BF16)) + _dot(a[i].astype(BF16), vv[i]))
    y = [_rmsnorm_rows(o[i], ng) * (zg[i] * _sigmoid(zg[i])) for i in heads]
    return y, st_in[len(zq):]


def _hgrn_kernel(q_ref, f_ref, i_ref, g_ref, lbp_ref, ng_ref, tri_ref, lvl_ref, o_ref, *, chunk, heads):
    c = chunk
    seq = q_ref.shape[0]
    p = lbp_ref[...]
    pe = jnp.exp(p - jnp.max(p, axis=0, keepdims=True))
    lb = pe[0:1, :] / jnp.sum(pe, axis=0, keepdims=True)
    oml = 1.0 - lb
    ng = ng_ref[...]
    row = lax.broadcasted_iota(jnp.int32, (c, LANES), 0)

    lanes = [slice(hh * LANES, (hh + 1) * LANES) for hh in range(heads)]

    nch = HG_CHUNKS_PER_STEP

    def body(ci, sts):
        ent = [(pl.ds(pl.multiple_of((ci * nch + ch) * c, c), c), ln) for ch in range(nch) for ln in lanes]
        y, new = _hgrn_chunk(
            [q_ref[rows, ln] for rows, ln in ent], [f_ref[rows, ln] for rows, ln in ent],
            [i_ref[rows, ln].astype(BF16) for rows, ln in ent], [g_ref[rows, ln] for rows, ln in ent],
            sts, [lb[:, ln] for _, ln in ent], [oml[:, ln] for _, ln in ent],
            ng, tri_ref[...], lvl_ref[...], row)
        for (rows, ln), yy in zip(ent, y):
            o_ref[rows, ln] = yy.astype(o_ref.dtype)
        return tuple(new)

    lax.fori_loop(0, seq // (c * nch), body, tuple(jnp.zeros((HG_DIM, HG_DIM), F32) for _ in range(heads)))


def _hgrn2(proj, lb_param, norm_g, batch):
    c, hp = HG_CHUNK, HG_HEADS_PER_STEP
    tri, lvl = _hgrn_tables(c)
    m = proj.shape[0]
    w = hp * LANES

    def col(off):
        return pl.BlockSpec((SEQ, w), lambda b, h, o=off // hp: (b, o + h))

    return pl.pallas_call(
        functools.partial(_hgrn_kernel, chunk=c, heads=hp),
        grid=(batch, HG_HEADS // hp),
        in_specs=[
            col(_OFF_HQ), col(_OFF_HF), col(_OFF_HI), col(_OFF_HG),
            pl.BlockSpec((lb_param.shape[0], w), lambda b, h: (0, h)),
            pl.BlockSpec((1, LANES), lambda b, h: (0, 0)),
            pl.BlockSpec((c, c), lambda b, h: (0, 0)),
            pl.BlockSpec((c, c), lambda b, h: (0, 0)),
        ],
        out_specs=pl.BlockSpec((SEQ, w), lambda b, h: (b, h)),
        out_shape=jax.ShapeDtypeStruct((m, HG_WIDTH), BF16),
        compiler_params=_params("parallel", "parallel"),
        name="hgrn2",
    )(proj, proj, proj, proj, lb_param, norm_g.reshape(1, LANES), tri, lvl)


ATTN_BLOCK = 128
ATTN_BATCH = 8


def _attn_kernel(q0, k0, v0, q1, k1, v1, q2, k2, v2, o_ref, og_ref, lse_ref):
    blk = ATTN_BLOCK
    scale = HEAD_DIM ** -0.5
    qi = lax.broadcasted_iota(jnp.int32, (blk, 2 * blk), 0)
    ki = lax.broadcasted_iota(jnp.int32, (blk, 2 * blk), 1)
    mask2 = (ki >= qi) & (ki <= qi + blk)
    mask1 = (lax.broadcasted_iota(jnp.int32, (blk, blk), 1)
             <= lax.broadcasted_iota(jnp.int32, (blk, blk), 0))

    def run_batch(items):
        s = [_dot_nt(qj, kw) * (scale * LOG2E) for qj, kw, _, _, _ in items]
        ps, mxs = [], []
        for (_, _, _, mask, _), sj in zip(items, s):
            sj = jnp.where(mask, sj, NEG_INF)
            mx = jnp.max(sj, axis=-1, keepdims=True)
            ps.append(jnp.exp2(sj - mx).astype(BF16))
            mxs.append(mx)
        os = [_dot(p, jnp.concatenate([vw, jnp.ones_like(vw)], axis=1)) for p, (_, _, vw, _, _) in zip(ps, items)]
        for (_, _, _, _, (g, rows)), o, mx in zip(items, os, mxs):
            den = o[:, HEAD_DIM:]
            og_ref.at[g][rows, :] = o[:, :HEAD_DIM] * (1.0 / den)
            lse_ref.at[g][rows, :] = mx + jnp.log(den) * LOG2E

    refs = ((q0, k0, v0), (q1, k1, v1), (q2, k2, v2))
    pending = []
    for g, (window, dil) in enumerate(ATTN_GROUPS):
        assert window // dil == blk
        q_ref, k_ref, v_ref = refs[g]
        length = SEQ // dil
        for r in range(dil):
            base = (_attn_slot(r) if dil == 16 else r) * length
            for j in range(length // blk):
                lo, hi = base + max(j - 1, 0) * blk, base + (j + 1) * blk
                rows = pl.ds(j * blk, blk) if dil == 1 else pl.ds(j * blk * dil + r, blk, stride=dil)
                pending.append((q_ref[hi - blk:hi, :], k_ref[lo:hi, :], v_ref[lo:hi, :],
                                mask1 if j == 0 else mask2, (g, rows)))
                if len(pending) == ATTN_BATCH:
                    run_batch(pending)
                    pending = []
    assert not pending

    l0, l1, l2 = lse_ref[0], lse_ref[1], lse_ref[2]
    mx = jnp.maximum(jnp.maximum(l0, l1), l2)
    w0, w1, w2 = jnp.exp2(l0 - mx), jnp.exp2(l1 - mx), jnp.exp2(l2 - mx)
    o = (w0 * og_ref[0] + w1 * og_ref[1] + w2 * og_ref[2]) / (w0 + w1 + w2)
    o_ref[...] = o.astype(o_ref.dtype)


def _attention(qkv, batch):
    m = qkv[0].shape[0]
    hg = ATTN_HEADS_PER_GROUP

    def col(part):
        return pl.BlockSpec((SEQ, LANES), lambda b, h, o=part * hg: (b, o + h))

    return pl.pallas_call(
        _attn_kernel,
        grid=(batch, hg),
        in_specs=[col(0), col(1), col(2)] * len(ATTN_GROUPS),
        out_specs=pl.BlockSpec((SEQ, LANES), lambda b, h: (b, h)),
        out_shape=jax.ShapeDtypeStruct((m, ATTN_OUT), BF16),
        scratch_shapes=[pltpu.VMEM((len(ATTN_GROUPS), SEQ, LANES), F32)] * 2,
        compiler_params=_params("parallel", "parallel"),
        name="dilated_attn",
    )(*[a for g in qkv for a in (g, g, g)])


MIX_GATE_COLS = 512


def _mix_kernel(*refs):
    nc = D_MODEL // MIX_GATE_COLS
    oa_ref, ob_ref = refs[0], refs[1]
    ga_refs, gb_refs = refs[2:2 + nc], refs[2 + nc:2 + 2 * nc]
    x_ref, wpa_ref, wpb_ref, wo_ref, n2_ref, h_ref, v_ref = refs[2 + 2 * nc:]
    oa, ob = oa_ref[...], ob_ref[...]
    parts = []
    for c in range(nc):
        cols = slice(c * MIX_GATE_COLS, (c + 1) * MIX_GATE_COLS)
        ya = _dot(oa, wpa_ref[:, cols])
        yb = _dot(ob, wpb_ref[:, cols])
        mix = jax.nn.sigmoid(ga_refs[c][...]) * ya + jax.nn.sigmoid(gb_refs[c][...]) * yb
        parts.append(mix.astype(BF16))
    h = x_ref[...] + _dot(jnp.concatenate(parts, axis=1), wo_ref[...])
    h_ref[...] = h
    v_ref[...] = _rmsnorm_rows(h, n2_ref[...]).astype(v_ref.dtype)


def _mix(oa, ob, proj, x2, wpa, wpb, wo, norm2_g, tm=256):
    m, d = x2.shape
    gc = MIX_GATE_COLS
    nc = d // gc
    resident = dict(pipeline_mode=pl.Buffered(1))

    def gate(off, c):
        return pl.BlockSpec((tm, gc), lambda i, o=off * LANES // gc + c: (i, o))

    return pl.pallas_call(
        _mix_kernel,
        grid=(m // tm,),
        in_specs=[
            pl.BlockSpec((tm, HG_WIDTH), lambda i: (i, 0)),
            pl.BlockSpec((tm, ATTN_OUT), lambda i: (i, 0)),
            *[gate(_OFF_GA, c) for c in range(nc)],
            *[gate(_OFF_GB, c) for c in range(nc)],
            pl.BlockSpec((tm, d), lambda i: (i, 0)),
            pl.BlockSpec(wpa.shape, lambda i: (0, 0), **resident),
            pl.BlockSpec(wpb.shape, lambda i: (0, 0), **resident),
            pl.BlockSpec(wo.shape, lambda i: (0, 0), **resident),
            pl.BlockSpec((1, d), lambda i: (0, 0)),
        ],
        out_specs=[pl.BlockSpec((tm, d), lambda i: (i, 0)), pl.BlockSpec((tm, d), lambda i: (i, 0))],
        out_shape=[jax.ShapeDtypeStruct((m, d), F32), jax.ShapeDtypeStruct((m, d), BF16)],
        compiler_params=_params("parallel"),
        name="mix_outproj",
    )(oa, ob, *([proj] * (2 * nc)), x2, wpa, wpb, wo, norm2_g.reshape(1, d))


def _shift_rows(h, prev, k):
    r = pltpu.roll(h, k, axis=0)
    p = pltpu.roll(prev, k, axis=0)
    row8 = lax.broadcasted_iota(jnp.int32, prev.shape, 0)
    top = jnp.where(row8 < k, p, r[:8])
    return jnp.concatenate([top, r[8:]], axis=0)


def _causal_conv3(h, prev, w, b):
    return w[2:3] * h + w[1:2] * _shift_rows(h, prev, 1) + w[0:1] * _shift_rows(h, prev, 2) + b


def _gelu_tanh_gate(x, y):
    c = math.sqrt(2.0 / math.pi)
    w = x * ((-2.0 * c * LOG2E) + (-2.0 * c * 0.044715 * LOG2E) * (x * x))
    return (x * y) * (1.0 / (1.0 + jnp.exp2(w)))


def _ffn_up_kernel(v_ref, halo_ref, wa_ref, wb_ref, cwa_ref, cwb_ref, cba_ref, cbb_ref, o_ref, *, sub):
    tm = v_ref.shape[0]
    hr = halo_ref.shape[0]
    starts_sequence = (pl.program_id(0) * tm) % SEQ == 0
    halo = jnp.where(starts_sequence, jnp.zeros_like(halo_ref[...]), halo_ref[...])
    wa, wb = wa_ref[...].astype(BF16), wb_ref[...].astype(BF16)
    cwa, cwb, cba, cbb = cwa_ref[...], cwb_ref[...], cba_ref[...], cbb_ref[...]
    pa = pb = None
    for r0 in range(0, tm, sub):
        vs = v_ref[r0:r0 + sub, :]
        if r0 == 0:
            vs = jnp.concatenate([halo, vs], axis=0)
        ha, hb = _dot(vs, wa), _dot(vs, wb)
        if r0 == 0:
            pa, pb = ha[hr - 8:hr], hb[hr - 8:hr]
            ha, hb = ha[hr:], hb[hr:]
        ca = _causal_conv3(ha, pa, cwa, cba)
        cb = _causal_conv3(hb, pb, cwb, cbb)
        o_ref[r0:r0 + sub, :] = _gelu_tanh_gate(ca, cb).astype(o_ref.dtype)
        pa, pb = ha[sub - 8:], hb[sub - 8:]


def _ffn_up(v, w_up, conv_w, conv_b, tm=2048, tn=512, sub=256):
    m, d = v.shape
    nj = D_FF // tn
    hb = tm // BF16_SUBLANES
    return pl.pallas_call(
        functools.partial(_ffn_up_kernel, sub=sub),
        grid=(m // tm, nj),
        in_specs=[
            pl.BlockSpec((tm, d), lambda i, j: (i, 0)),
            pl.BlockSpec((BF16_SUBLANES, d), lambda i, j: (jnp.maximum(i * hb - 1, 0), 0)),
            pl.BlockSpec((d, tn), lambda i, j: (0, j)),
            pl.BlockSpec((d, tn), lambda i, j: (0, j + nj)),
            pl.BlockSpec((3, tn), lambda i, j: (0, j)),
            pl.BlockSpec((3, tn), lambda i, j: (0, j + nj)),
            pl.BlockSpec((1, tn), lambda i, j: (0, j)),
            pl.BlockSpec((1, tn), lambda i, j: (0, j + nj)),
        ],
        out_specs=pl.BlockSpec((tm, tn), lambda i, j: (i, j)),
        out_shape=jax.ShapeDtypeStruct((m, D_FF), BF16),
        compiler_params=_params("parallel", "parallel"),
        name="ffn_up_conv_gate",
    )(v, v, w_up, w_up, conv_w, conv_w, conv_b.reshape(1, -1), conv_b.reshape(1, -1))


def _ffn_down_kernel(a_ref, w_ref, h_ref, g_ref, o_ref):
    o_ref[...] = _rmsnorm_rows(h_ref[...] + _dot(a_ref[...], w_ref[...]), g_ref[...])


def _ffn_down(act, w_down, h1, norm_g, tm=256):
    m, kdim = act.shape
    d = w_down.shape[1]
    return pl.pallas_call(
        _ffn_down_kernel,
        grid=(m // tm,),
        in_specs=[
            pl.BlockSpec((tm, kdim), lambda i: (i, 0)),
            pl.BlockSpec((kdim, d), lambda i: (0, 0), pipeline_mode=pl.Buffered(1)),
            pl.BlockSpec((tm, d), lambda i: (i, 0)),
            pl.BlockSpec((1, d), lambda i: (0, 0)),
        ],
        out_specs=pl.BlockSpec((tm, d), lambda i: (i, 0)),
        out_shape=jax.ShapeDtypeStruct((m, d), F32),
        compiler_params=_params("parallel"),
        name="ffn_down_norm",
    )(act, w_down, h1, norm_g.reshape(1, d))


def kernel(x, norm1_g, w_in, hg_lb_param, hg_norm_g, w_proj_a, w_proj_b, w_out,
           norm2_g, w_up, conv_w, conv_b, w_down, norm_f_g):
    batch, seq, d = x.shape
    depth = w_in.shape[0]
    assert (seq, d, depth) == (SEQ, D_MODEL, 1) and w_in.shape[2] == IN_COLS
    h = x.reshape(batch * seq, d)
    for l in range(depth):
        u = _norm_bf16(h, norm1_g[l])
        proj, qkv, (wpa, wpb, wo, wdn) = _in_proj(u, w_in[l], (w_proj_a[l], w_proj_b[l], w_out[l], w_down[l]))
        oa = _hgrn2(proj, hg_lb_param, hg_norm_g[l], batch)
        ob = _attention(qkv, batch)
        h, v = _mix(oa, ob, proj, h, wpa, wpb, wo, norm2_g[l])
        act = _ffn_up(v, w_up[l], conv_w[l], conv_b[l])
        out = _ffn_down(act, wdn, h, norm_f_g)
    return out.reshape(batch, seq, d)
```

```python
import functools
import math

import numpy as np
import jax
import jax.numpy as jnp
from jax import lax
from jax.experimental import pallas as pl
from jax.experimental.pallas import tpu as pltpu

D_MODEL = 2048
SEQ = 2048
HG_HEADS = 16
HG_DIM = 128
HG_WIDTH = HG_HEADS * HG_DIM
ATTN_GROUPS = ((128, 1), (512, 4), (2048, 16))
ATTN_HEADS_PER_GROUP = 4
HEAD_DIM = 128
ATTN_WIDTH = len(ATTN_GROUPS) * ATTN_HEADS_PER_GROUP * HEAD_DIM
ATTN_OUT = ATTN_HEADS_PER_GROUP * HEAD_DIM
ROPE_THETA = 10000.0
D_FF = 5632
NORM_EPS = 1e-6
NEG_INF = -1e30
IN_COLS = 4 * HG_WIDTH + 3 * ATTN_WIDTH + 2 * D_MODEL

_OFF_HQ = 0
_OFF_HF = HG_WIDTH // 128
_OFF_HI = 2 * HG_WIDTH // 128
_OFF_HG = 3 * HG_WIDTH // 128
_OFF_AQ = 4 * HG_WIDTH // 128
_OFF_AK = _OFF_AQ + ATTN_WIDTH // 128
_OFF_AV = _OFF_AK + ATTN_WIDTH // 128
_OFF_GA = 4 * HG_WIDTH // 128
_OFF_GB = _OFF_GA + D_MODEL // 128

LANES = 128
BF16_SUBLANES = 16
VMEM_LIMIT = 56 * 1024 * 1024
LOG2E = 1.4426950408889634

F32 = jnp.float32
BF16 = jnp.bfloat16


def _dot(a, b):
    return jnp.dot(a, b, preferred_element_type=F32)


def _dot_nt(a, b):
    return lax.dot_general(a, b, (((1,), (1,)), ((), ())), preferred_element_type=F32)


def _dot_tn(a, b):
    return lax.dot_general(a, b, (((0,), (0,)), ((), ())), preferred_element_type=F32)


def _rmsnorm_rows(x, g):
    ms = jnp.mean(x * x, axis=-1, keepdims=True)
    return x * lax.rsqrt(ms + NORM_EPS) * g


def _sigmoid(x):
    return 1.0 / (1.0 + jnp.exp2(x * (-LOG2E)))


def _params(*sem):
    return pltpu.CompilerParams(dimension_semantics=sem, vmem_limit_bytes=VMEM_LIMIT)


def _norm_kernel(x_ref, g_ref, u_ref):
    u_ref[...] = _rmsnorm_rows(x_ref[...], g_ref[...]).astype(u_ref.dtype)


def _norm_bf16(x2, norm_g, tm=512):
    m, d = x2.shape
    return pl.pallas_call(
        _norm_kernel,
        grid=(m // tm,),
        in_specs=[pl.BlockSpec((tm, d), lambda i: (i, 0)), pl.BlockSpec((1, d), lambda i: (0, 0))],
        out_specs=pl.BlockSpec((tm, d), lambda i: (i, 0)),
        out_shape=jax.ShapeDtypeStruct((m, d), BF16),
        compiler_params=_params("parallel"),
        name="norm1",
    )(x2, norm_g.reshape(1, d))


def _attn_tile(u_ref, w_ref, cos_ref, sin_ref, o_ref, scr, scr2, *, dil, sub):
    w = w_ref[...].astype(BF16)
    tm = u_ref.shape[0]
    length, n = tm // dil, sub // dil
    for r0 in range(0, tm, sub):
        acc = _dot(u_ref[r0:r0 + sub, :], w)
        cs, sn = cos_ref[r0:r0 + sub, :], sin_ref[r0:r0 + sub, :]
        for hh in range(acc.shape[1] // HEAD_DIM):
            ln = slice(hh * HEAD_DIM, (hh + 1) * HEAD_DIM)
            x = acc[:, ln]
            y = x * cs + pltpu.roll(x, HEAD_DIM // 2, axis=1) * sn
            if dil == 1:
                o_ref[r0:r0 + sub, ln] = y.astype(o_ref.dtype)
                continue
            scr[hh] = y
            for r1 in range(4):
                part = scr[hh, pl.ds(r1, sub // 4, stride=4), :]
                if dil == 4:
                    dst = r1 * length + r0 // dil
                    o_ref[dst:dst + n, ln] = part.astype(o_ref.dtype)
                    continue
                scr2[hh, r1] = part
                for r2 in range(4):
                    dst = _attn_slot(r1 + 4 * r2) * length + r0 // dil
                    o_ref[dst:dst + n, ln] = scr2[hh, r1, pl.ds(r2, n, stride=4), :].astype(o_ref.dtype)


def _attn_slot(r):
    return (r % 4) * 4 + r // 4


def _inproj_kernel(u_ref, w_ref, cos_ref, sin_ref, *refs, n_side, attn_lo, sub):
    side_in, o_main = refs[:n_side], refs[n_side]
    qkv = refs[n_side + 1:n_side + 1 + len(ATTN_GROUPS)]
    side_out = refs[n_side + 1 + len(ATTN_GROUPS):-2]
    scr, scr2 = refs[-2:]
    ng = len(ATTN_GROUPS)
    ja = pl.program_id(1) - attn_lo
    is_attn = (ja >= 0) & (ja < 3 * ng)

    @pl.when(jnp.logical_not(is_attn))
    def _():
        o_main[...] = _dot(u_ref[...], w_ref[...].astype(BF16))

    for g, (_, dil) in enumerate(ATTN_GROUPS):
        @pl.when(is_attn & (lax.rem(ja, ng) == g))
        def _():
            _attn_tile(u_ref, w_ref, cos_ref, sin_ref, qkv[g], scr, scr2, dil=dil, sub=sub)

    for src, dst in zip(side_in, side_out):
        dst[...] = src[...].astype(dst.dtype)


def _in_proj(u, w, side_weights, tm=SEQ, tn=512, sub=512):
    m, d = u.shape
    ng = len(ATTN_GROUPS)
    assert tn == ATTN_HEADS_PER_GROUP * HEAD_DIM and all(dl in (1, 4, 16) for _, dl in ATTN_GROUPS)
    attn_lo = 4 * HG_WIDTH // tn
    n_attn = 3 * ng
    nj = w.shape[1] // tn
    n_main = nj - n_attn
    steps = (m // tm) * nj

    half = HEAD_DIM // 2
    inv_freq = jnp.exp(-math.log(ROPE_THETA) * jnp.arange(half, dtype=F32) * (2.0 / HEAD_DIM))
    ang = jnp.arange(SEQ, dtype=F32)[:, None] * inv_freq[None, :]
    cos, sin = jnp.cos(ang), jnp.sin(ang)
    one, zero = jnp.ones((SEQ, HEAD_DIM), F32), jnp.zeros((SEQ, HEAD_DIM), F32)
    cos_t = jnp.stack([jnp.concatenate([cos, cos], axis=1), one])
    sin_t = jnp.stack([jnp.concatenate([-sin, sin], axis=1), zero])

    def part(j):
        return jnp.clip((j - attn_lo) // ng, 0, 2)

    table = pl.BlockSpec((None, SEQ, HEAD_DIM), lambda i, j: (part(j) // 2, 0, 0), pipeline_mode=pl.Buffered(1))

    def main_col(j):
        return jnp.where(j < attn_lo, j, jnp.maximum(j - n_attn, attn_lo - 1))

    def qkv_spec(g):
        return pl.BlockSpec((tm, tn), lambda i, j: (i, jnp.clip((j - attn_lo - g) // ng, 0, 2)))

    side_specs, side_shape = [], []
    for sw in side_weights:
        rows = -(-sw.shape[0] // steps)
        rows = -(-rows // BF16_SUBLANES) * BF16_SUBLANES
        while sw.shape[0] % rows:
            rows += BF16_SUBLANES
        last = sw.shape[0] // rows - 1
        side_specs.append(pl.BlockSpec((rows, sw.shape[1]), lambda i, j, last=last: (jnp.minimum(i * nj + j, last), 0)))
        side_shape.append(jax.ShapeDtypeStruct(sw.shape, BF16))
    heads = tn // HEAD_DIM
    outs = pl.pallas_call(
        functools.partial(_inproj_kernel, n_side=len(side_weights), attn_lo=attn_lo, sub=sub),
        grid=(m // tm, nj),
        in_specs=[
            pl.BlockSpec((tm, d), lambda i, j: (i, 0)),
            pl.BlockSpec((d, tn), lambda i, j: (0, j)),
            table, table,
            *side_specs,
        ],
        out_specs=[pl.BlockSpec((tm, tn), lambda i, j: (i, main_col(j))),
                   *[qkv_spec(g) for g in range(ng)], *side_specs],
        out_shape=[jax.ShapeDtypeStruct((m, n_main * tn), F32),
                   *[jax.ShapeDtypeStruct((m, 3 * tn), BF16)] * ng, *side_shape],
        scratch_shapes=[pltpu.VMEM((heads, sub, HEAD_DIM), F32), pltpu.VMEM((heads, 4, sub // 4, HEAD_DIM), F32)],
        compiler_params=_params("arbitrary", "arbitrary"),
        name="in_proj",
    )(u, w, cos_t, sin_t, *side_weights)
    return outs[0], outs[1:1 + ng], outs[1 + ng:]


HG_CHUNK = 128
HG_HEADS_PER_STEP = 4
HG_CHUNKS_PER_STEP = 4


def _hgrn_tables(c):
    t = np.arange(c)[:, None]
    s = np.arange(c)[None, :]
    x = t ^ s
    lvl = np.where(x > 0, np.floor(np.log2(np.maximum(x, 1))), -1).astype(np.int32)
    lvl = np.where(s > t, -2, lvl).astype(np.int32)
    tri = (s <= t).astype(np.float32)
    return jnp.asarray(tri, BF16), jnp.asarray(lvl, jnp.int32)


def _neg_abs(d):
    bits = lax.bitcast_convert_type(d, jnp.uint32) | jnp.uint32(0x80000000)
    return lax.bitcast_convert_type(bits, F32)


def _hgrn_chunk(zq, zf, vv, zg, sts, lb, oml, ng, tri, lvl, row):
    heads = range(len(zq))
    c = zq[0].shape[0]
    f = [lb[i] + oml[i] * _sigmoid(zf[i]) for i in heads]
    kk = [1.0 - f[i] for i in heads]
    qq = [zq[i] * _sigmoid(zq[i]) for i in heads]
    gl = [jnp.log(f[i]) * LOG2E for i in heads]

    gcum = []
    for i in heads:
        g1 = gl[i].astype(BF16)
        g2 = (gl[i] - g1.astype(F32)).astype(BF16)
        gc = _dot(tri, jnp.concatenate([g1, g2], axis=1))
        gcum.append(gc[:, :LANES] + gc[:, LANES:])
    glast = [gcum[i][c - 1:c, :] for i in heads]

    qb = [qq[i].astype(BF16) for i in heads]
    kb = [kk[i].astype(BF16) for i in heads]
    st_in, a = list(sts), []
    for i in heads:
        kd = (kk[i] * jnp.exp2(glast[i] - gcum[i])).astype(BF16)
        st_in.append(st_in[i] * jnp.exp2(glast[i]) + _dot_tn(vv[i], kd))
        a.append(jnp.where(lvl == -1, _dot_nt(qb[i], kb[i]), 0.0))

    h = c // 2
    while h >= 1:
        in_level = lvl == int(math.log2(h))
        for i in heads:
            if h >= 4:
                g3d = gcum[i].reshape(c // (2 * h), 2 * h, LANES)
                ref = jnp.broadcast_to(g3d[:, h - 1:h, :], g3d.shape).reshape(c, LANES)
                x = _neg_abs(gcum[i] - ref)
            elif h == 2:
                pos = row & 3
                up = pltpu.roll(gl[i], c - 1, axis=0)
                dn = pltpu.roll(gl[i], 1, axis=0)
                x = jnp.where(pos == 0, up, jnp.where(pos == 1, 0.0, jnp.where(pos == 2, gl[i], gl[i] + dn)))
            else:
                x = jnp.where((row & 1) == 1, gl[i], 0.0)
            e = jnp.exp2(x).astype(BF16)
            a[i] = jnp.where(in_level, _dot_nt(qb[i] * e, kb[i] * e), a[i])
        h //= 2
    o = []
    for i in heads:
        qg = (qq[i] * jnp.exp2(gcum[i])).astype(BF16)
        o.append(_dot_nt(qg, st_in[i].astype(BF16)) + _dot(a[i].astype(BF16), vv[i]))
    y = [_rmsnorm_rows(o[i], ng) * (zg[i] * _sigmoid(zg[i])) for i in heads]
    return y, st_in[len(zq):]


def _hgrn_kernel(q_ref, f_ref, i_ref, g_ref, lbp_ref, ng_ref, tri_ref, lvl_ref, o_ref, *, chunk, heads):
    c = chunk
    seq = q_ref.shape[0]
    p = lbp_ref[...]
    pe = jnp.exp(p - jnp.max(p, axis=0, keepdims=True))
    lb = pe[0:1, :] / jnp.sum(pe, axis=0, keepdims=True)
    oml = 1.0 - lb
    ng = ng_ref[...]
    row = lax.broadcasted_iota(jnp.int32, (c, LANES), 0)

    lanes = [slice(hh * LANES, (hh + 1) * LANES) for hh in range(heads)]

    nch = HG_CHUNKS_PER_STEP

    def body(ci, sts):
        ent = [(pl.ds(pl.multiple_of((ci * nch + ch) * c, c), c), ln) for ch in range(nch) for ln in lanes]
        y, new = _hgrn_chunk(
            [q_ref[rows, ln] for rows, ln in ent], [f_ref[rows, ln] for rows, ln in ent],
            [i_ref[rows, ln].astype(BF16) for rows, ln in ent], [g_ref[rows, ln] for rows, ln in ent],
            sts, [lb[:, ln] for _, ln in ent], [oml[:, ln] for _, ln in ent],
            ng, tri_ref[...], lvl_ref[...], row)
        for (rows, ln), yy in zip(ent, y):
            o_ref[rows, ln] = yy.astype(o_ref.dtype)
        return tuple(new)

    lax.fori_loop(0, seq // (c * nch), body, tuple(jnp.zeros((HG_DIM, HG_DIM), F32) for _ in range(heads)))


def _hgrn2(proj, lb_param, norm_g, batch):
    c, hp = HG_CHUNK, HG_HEADS_PER_STEP
    tri, lvl = _hgrn_tables(c)
    m = proj.shape[0]
    w = hp * LANES

    def col(off):
        return pl.BlockSpec((SEQ, w), lambda b, h, o=off // hp: (b, o + h))

    return pl.pallas_call(
        functools.partial(_hgrn_kernel, chunk=c, heads=hp),
        grid=(batch, HG_HEADS // hp),
        in_specs=[
            col(_OFF_HQ), col(_OFF_HF), col(_OFF_HI), col(_OFF_HG),
            pl.BlockSpec((lb_param.shape[0], w), lambda b, h: (0, h)),
            pl.BlockSpec((1, LANES), lambda b, h: (0, 0)),
            pl.BlockSpec((c, c), lambda b, h: (0, 0)),
            pl.BlockSpec((c, c), lambda b, h: (0, 0)),
        ],
        out_specs=pl.BlockSpec((SEQ, w), lambda b, h: (b, h)),
        out_shape=jax.ShapeDtypeStruct((m, HG_WIDTH), BF16),
        compiler_params=_params("parallel", "parallel"),
        name="hgrn2",
    )(proj, proj, proj, proj, lb_param, norm_g.reshape(1, LANES), tri, lvl)


ATTN_BLOCK = 128
ATTN_BATCH = 8


def _attn_kernel(q0, k0, v0, q1, k1, v1, q2, k2, v2, o_ref, og_ref, lse_ref):
    blk = ATTN_BLOCK
    scale = HEAD_DIM ** -0.5
    qi = lax.broadcasted_iota(jnp.int32, (blk, 2 * blk), 0)
    ki = lax.broadcasted_iota(jnp.int32, (blk, 2 * blk), 1)
    mask2 = (ki >= qi) & (ki <= qi + blk)
    mask1 = (lax.broadcasted_iota(jnp.int32, (blk, blk), 1)
             <= lax.broadcasted_iota(jnp.int32, (blk, blk), 0))

    def run_batch(items):
        s = [_dot_nt(qj, kw) * (scale * LOG2E) for qj, kw, _, _, _ in items]
        ps, mxs = [], []
        for (_, _, _, mask, _), sj in zip(items, s):
            sj = jnp.where(mask, sj, NEG_INF)
            mx = jnp.max(sj, axis=-1, keepdims=True)
            ps.append(jnp.exp2(sj - mx).astype(BF16))
            mxs.append(mx)
        os = [_dot(p, jnp.concatenate([vw, jnp.ones_like(vw)], axis=1)) for p, (_, _, vw, _, _) in zip(ps, items)]
        for (_, _, _, _, (g, rows)), o, mx in zip(items, os, mxs):
            den = o[:, HEAD_DIM:]
            og_ref.at[g][rows, :] = o[:, :HEAD_DIM] * (1.0 / den)
            lse_ref.at[g][rows, :] = mx + jnp.log(den) * LOG2E

    refs = ((q0, k0, v0), (q1, k1, v1), (q2, k2, v2))
    pending = []
    for g, (window, dil) in enumerate(ATTN_GROUPS):
        assert window // dil == blk
        q_ref, k_ref, v_ref = refs[g]
        length = SEQ // dil
        for r in range(dil):
            base = (_attn_slot(r) if dil == 16 else r) * length
            for j in range(length // blk):
                lo, hi = base + max(j - 1, 0) * blk, base + (j + 1) * blk
                rows = pl.ds(j * blk, blk) if dil == 1 else pl.ds(j * blk * dil + r, blk, stride=dil)
                pending.append((q_ref[hi - blk:hi, :], k_ref[lo:hi, :], v_ref[lo:hi, :],
                                mask1 if j == 0 else mask2, (g, rows)))
                if len(pending) == ATTN_BATCH:
                    run_batch(pending)
                    pending = []
    assert not pending

    l0, l1, l2 = lse_ref[0], lse_ref[1], lse_ref[2]
    mx = jnp.maximum(jnp.maximum(l0, l1), l2)
    w0, w1, w2 = jnp.exp2(l0 - mx), jnp.exp2(l1 - mx), jnp.exp2(l2 - mx)
    o = (w0 * og_ref[0] + w1 * og_ref[1] + w2 * og_ref[2]) / (w0 + w1 + w2)
    o_ref[...] = o.astype(o_ref.dtype)


def _attention(qkv, batch):
    m = qkv[0].shape[0]
    hg = ATTN_HEADS_PER_GROUP

    def col(part):
        return pl.BlockSpec((SEQ, LANES), lambda b, h, o=part * hg: (b, o + h))

    return pl.pallas_call(
        _attn_kernel,
        grid=(batch, hg),
        in_specs=[col(0), col(1), col(2)] * len(ATTN_GROUPS),
        out_specs=pl.BlockSpec((SEQ, LANES), lambda b, h: (b, h)),
        out_shape=jax.ShapeDtypeStruct((m, ATTN_OUT), BF16),
        scratch_shapes=[pltpu.VMEM((len(ATTN_GROUPS), SEQ, LANES), F32)] * 2,
        compiler_params=_params("parallel", "parallel"),
        name="dilated_attn",
    )(*[a for g in qkv for a in (g, g, g)])


MIX_GATE_COLS = 512


def _mix_kernel(*refs):
    nc = D_MODEL // MIX_GATE_COLS
    oa_ref, ob_ref = refs[0], refs[1]
    ga_refs, gb_refs = refs[2:2 + nc], refs[2 + nc:2 + 2 * nc]
    x_ref, wpa_ref, wpb_ref, wo_ref, n2_ref, h_ref, v_ref = refs[2 + 2 * nc:]
    oa, ob = oa_ref[...], ob_ref[...]
    parts = []
    for c in range(nc):
        cols = slice(c * MIX_GATE_COLS, (c + 1) * MIX_GATE_COLS)
        ya = _dot(oa, wpa_ref[:, cols])
        yb = _dot(ob, wpb_ref[:, cols])
        mix = jax.nn.sigmoid(ga_refs[c][...]) * ya + jax.nn.sigmoid(gb_refs[c][...]) * yb
        parts.append(mix.astype(BF16))
    h = x_ref[...] + _dot(jnp.concatenate(parts, axis=1), wo_ref[...])
    h_ref[...] = h
    v_ref[...] = _rmsnorm_rows(h, n2_ref[...]).astype(v_ref.dtype)


def _mix(oa, ob, proj, x2, wpa, wpb, wo, norm2_g, tm=256):
    m, d = x2.shape
    gc = MIX_GATE_COLS
    nc = d // gc
    resident = dict(pipeline_mode=pl.Buffered(1))

    def gate(off, c):
        return pl.BlockSpec((tm, gc), lambda i, o=off * LANES // gc + c: (i, o))

    return pl.pallas_call(
        _mix_kernel,
        grid=(m // tm,),
        in_specs=[
            pl.BlockSpec((tm, HG_WIDTH), lambda i: (i, 0)),
            pl.BlockSpec((tm, ATTN_OUT), lambda i: (i, 0)),
            *[gate(_OFF_GA, c) for c in range(nc)],
            *[gate(_OFF_GB, c) for c in range(nc)],
            pl.BlockSpec((tm, d), lambda i: (i, 0)),
            pl.BlockSpec(wpa.shape, lambda i: (0, 0), **resident),
            pl.BlockSpec(wpb.shape, lambda i: (0, 0), **resident),
            pl.BlockSpec(wo.shape, lambda i: (0, 0), **resident),
            pl.BlockSpec((1, d), lambda i: (0, 0)),
        ],
        out_specs=[pl.BlockSpec((tm, d), lambda i: (i, 0)), pl.BlockSpec((tm, d), lambda i: (i, 0))],
        out_shape=[jax.ShapeDtypeStruct((m, d), F32), jax.ShapeDtypeStruct((m, d), BF16)],
        compiler_params=_params("parallel"),
        name="mix_outproj",
    )(oa, ob, *([proj] * (2 * nc)), x2, wpa, wpb, wo, norm2_g.reshape(1, d))


def _shift_rows(h, prev, k):
    r = pltpu.roll(h, k, axis=0)
    p = pltpu.roll(prev, k, axis=0)
    row8 = lax.broadcasted_iota(jnp.int32, prev.shape, 0)
    top = jnp.where(row8 < k, p, r[:8])
    return jnp.concatenate([top, r[8:]], axis=0)


def _causal_conv3(h, prev, w, b):
    return w[2:3] * h + w[1:2] * _shift_rows(h, prev, 1) + w[0:1] * _shift_rows(h, prev, 2) + b


def _gelu_tanh_gate(x, y):
    c = math.sqrt(2.0 / math.pi)
    w = x * ((-2.0 * c * LOG2E) + (-2.0 * c * 0.044715 * LOG2E) * (x * x))
    return (x * y) * (1.0 / (1.0 + jnp.exp2(w)))


def _ffn_up_kernel(v_ref, halo_ref, wa_ref, wb_ref, cwa_ref, cwb_ref, cba_ref, cbb_ref, o_ref, *, sub):
    tm = v_ref.shape[0]
    hr = halo_ref.shape[0]
    starts_sequence = (pl.program_id(0) * tm) % SEQ == 0
    halo = jnp.where(starts_sequence, jnp.zeros_like(halo_ref[...]), halo_ref[...])
    wa, wb = wa_ref[...].astype(BF16), wb_ref[...].astype(BF16)
    cwa, cwb, cba, cbb = cwa_ref[...], cwb_ref[...], cba_ref[...], cbb_ref[...]
    pa = pb = None
    for r0 in range(0, tm, sub):
        vs = v_ref[r0:r0 + sub, :]
        if r0 == 0:
            vs = jnp.concatenate([halo, vs], axis=0)
        ha, hb = _dot(vs, wa), _dot(vs, wb)
        if r0 == 0:
            pa, pb = ha[hr - 8:hr], hb[hr - 8:hr]
            ha, hb = ha[hr:], hb[hr:]
        ca = _causal_conv3(ha, pa, cwa, cba)
        cb = _causal_conv3(hb, pb, cwb, cbb)
        o_ref[r0:r0 + sub, :] = _gelu_tanh_gate(ca, cb).astype(o_ref.dtype)
        pa, pb = ha[sub - 8:], hb[sub - 8:]


def _ffn_up(v, w_up, conv_w, conv_b, tm=2048, tn=512, sub=512):
    m, d = v.shape
    nj = D_FF // tn
    hb = tm // BF16_SUBLANES
    return pl.pallas_call(
        functools.partial(_ffn_up_kernel, sub=sub),
        grid=(m // tm, nj),
        in_specs=[
            pl.BlockSpec((tm, d), lambda i, j: (i, 0)),
            pl.BlockSpec((BF16_SUBLANES, d), lambda i, j: (jnp.maximum(i * hb - 1, 0), 0)),
            pl.BlockSpec((d, tn), lambda i, j: (0, j)),
            pl.BlockSpec((d, tn), lambda i, j: (0, j + nj)),
            pl.BlockSpec((3, tn), lambda i, j: (0, j)),
            pl.BlockSpec((3, tn), lambda i, j: (0, j + nj)),
            pl.BlockSpec((1, tn), lambda i, j: (0, j)),
            pl.BlockSpec((1, tn), lambda i, j: (0, j + nj)),
        ],
        out_specs=pl.BlockSpec((tm, tn), lambda i, j: (i, j)),
        out_shape=jax.ShapeDtypeStruct((m, D_FF), BF16),
        compiler_params=_params("parallel", "parallel"),
        name="ffn_up_conv_gate",
    )(v, v, w_up, w_up, conv_w, conv_w, conv_b.reshape(1, -1), conv_b.reshape(1, -1))


def _ffn_down_kernel(a_ref, w_ref, h_ref, g_ref, o_ref):
    o_ref[...] = _rmsnorm_rows(h_ref[...] + _dot(a_ref[...], w_ref[...]), g_ref[...])


def _ffn_down(act, w_down, h1, norm_g, tm=256):
    m, kdim = act.shape
    d = w_down.shape[1]
    return pl.pallas_call(
        _ffn_down_kernel,
        grid=(m // tm,),
        in_specs=[
            pl.BlockSpec((tm, kdim), lambda i: (i, 0)),
            pl.BlockSpec((kdim, d), lambda i: (0, 0), pipeline_mode=pl.Buffered(1)),
            pl.BlockSpec((tm, d), lambda i: (i, 0)),
            pl.BlockSpec((1, d), lambda i: (0, 0)),
        ],
        out_specs=pl.BlockSpec((tm, d), lambda i: (i, 0)),
        out_shape=jax.ShapeDtypeStruct((m, d), F32),
        compiler_params=_params("parallel"),
        name="ffn_down_norm",
    )(act, w_down, h1, norm_g.reshape(1, d))


def kernel(x, norm1_g, w_in, hg_lb_param, hg_norm_g, w_proj_a, w_proj_b, w_out,
           norm2_g, w_up, conv_w, conv_b, w_down, norm_f_g):
    batch, seq, d = x.shape
    depth = w_in.shape[0]
    assert (seq, d, depth) == (SEQ, D_MODEL, 1) and w_in.shape[2] == IN_COLS
    h = x.reshape(batch * seq, d)
    for l in range(depth):
        u = _norm_bf16(h, norm1_g[l])
        proj, qkv, (wpa, wpb, wo, wdn) = _in_proj(u, w_in[l], (w_proj_a[l], w_proj_b[l], w_out[l], w_down[l]))
        oa = _hgrn2(proj, hg_lb_param, hg_norm_g[l], batch)
        ob = _attention(qkv, batch)
        h, v = _mix(oa, ob, proj, h, wpa, wpb, wo, norm2_g[l])
        act = _ffn_up(v, w_up[l], conv_w[l], conv_b[l])
        out = _ffn_down(act, wdn, h, norm_f_g)
    return out.reshape(batch, seq, d)
```

```python
import functools
import math

import numpy as np
import jax
import jax.numpy as jnp
from jax import lax
from jax.experimental import pallas as pl
from jax.experimental.pallas import tpu as pltpu

D_MODEL = 2048
SEQ = 2048
HG_HEADS = 16
HG_DIM = 128
HG_WIDTH = HG_HEADS * HG_DIM
ATTN_GROUPS = ((128, 1), (512, 4), (2048, 16))
ATTN_HEADS_PER_GROUP = 4
HEAD_DIM = 128
ATTN_WIDTH = len(ATTN_GROUPS) * ATTN_HEADS_PER_GROUP * HEAD_DIM
ATTN_OUT = ATTN_HEADS_PER_GROUP * HEAD_DIM
ROPE_THETA = 10000.0
D_FF = 5632
NORM_EPS = 1e-6
NEG_INF = -1e30
IN_COLS = 4 * HG_WIDTH + 3 * ATTN_WIDTH + 2 * D_MODEL

_OFF_HQ = 0
_OFF_HF = HG_WIDTH // 128
_OFF_HI = 2 * HG_WIDTH // 128
_OFF_HG = 3 * HG_WIDTH // 128
_OFF_AQ = 4 * HG_WIDTH // 128
_OFF_AK = _OFF_AQ + ATTN_WIDTH // 128
_OFF_AV = _OFF_AK + ATTN_WIDTH // 128
_OFF_GA = 4 * HG_WIDTH // 128
_OFF_GB = _OFF_GA + D_MODEL // 128

LANES = 128
BF16_SUBLANES = 16
VMEM_LIMIT = 56 * 1024 * 1024
LOG2E = 1.4426950408889634

F32 = jnp.float32
BF16 = jnp.bfloat16


def _dot(a, b):
    return jnp.dot(a, b, preferred_element_type=F32)


def _dot_nt(a, b):
    return lax.dot_general(a, b, (((1,), (1,)), ((), ())), preferred_element_type=F32)


def _dot_tn(a, b):
    return lax.dot_general(a, b, (((0,), (0,)), ((), ())), preferred_element_type=F32)


def _rmsnorm_rows(x, g):
    ms = jnp.mean(x * x, axis=-1, keepdims=True)
    return x * lax.rsqrt(ms + NORM_EPS) * g


def _sigmoid(x):
    return 1.0 / (1.0 + jnp.exp2(x * (-LOG2E)))


def _params(*sem):
    return pltpu.CompilerParams(dimension_semantics=sem, vmem_limit_bytes=VMEM_LIMIT)


def _norm_kernel(x_ref, g_ref, u_ref):
    u_ref[...] = _rmsnorm_rows(x_ref[...], g_ref[...]).astype(u_ref.dtype)


def _norm_bf16(x2, norm_g, tm=512):
    m, d = x2.shape
    return pl.pallas_call(
        _norm_kernel,
        grid=(m // tm,),
        in_specs=[pl.BlockSpec((tm, d), lambda i: (i, 0)), pl.BlockSpec((1, d), lambda i: (0, 0))],
        out_specs=pl.BlockSpec((tm, d), lambda i: (i, 0)),
        out_shape=jax.ShapeDtypeStruct((m, d), BF16),
        compiler_params=_params("parallel"),
        name="norm1",
    )(x2, norm_g.reshape(1, d))


def _attn_tile(u_ref, w_ref, cos_ref, sin_ref, o_ref, scr, scr2, *, dil, sub):
    w = w_ref[...].astype(BF16)
    tm = u_ref.shape[0]
    length, n = tm // dil, sub // dil
    for r0 in range(0, tm, sub):
        acc = _dot(u_ref[r0:r0 + sub, :], w)
        cs, sn = cos_ref[r0:r0 + sub, :], sin_ref[r0:r0 + sub, :]
        for hh in range(acc.shape[1] // HEAD_DIM):
            ln = slice(hh * HEAD_DIM, (hh + 1) * HEAD_DIM)
            x = acc[:, ln]
            y = x * cs + pltpu.roll(x, HEAD_DIM // 2, axis=1) * sn
            if dil == 1:
                o_ref[r0:r0 + sub, ln] = y.astype(o_ref.dtype)
                continue
            scr[hh] = y
            for r1 in range(4):
                part = scr[hh, pl.ds(r1, sub // 4, stride=4), :]
                if dil == 4:
                    dst = r1 * length + r0 // dil
                    o_ref[dst:dst + n, ln] = part.astype(o_ref.dtype)
                    continue
                scr2[hh, r1] = part
                for r2 in range(4):
                    dst = _attn_slot(r1 + 4 * r2) * length + r0 // dil
                    o_ref[dst:dst + n, ln] = scr2[hh, r1, pl.ds(r2, n, stride=4), :].astype(o_ref.dtype)


def _attn_slot(r):
    return (r % 4) * 4 + r // 4


def _inproj_kernel(u_ref, w_ref, cos_ref, sin_ref, *refs, n_side, attn_lo, sub):
    side_in, o_main = refs[:n_side], refs[n_side]
    qkv = refs[n_side + 1:n_side + 1 + len(ATTN_GROUPS)]
    side_out = refs[n_side + 1 + len(ATTN_GROUPS):-2]
    scr, scr2 = refs[-2:]
    ng = len(ATTN_GROUPS)
    ja = pl.program_id(1) - attn_lo
    is_attn = (ja >= 0) & (ja < 3 * ng)

    @pl.when(jnp.logical_not(is_attn))
    def _():
        o_main[...] = _dot(u_ref[...], w_ref[...].astype(BF16))

    for g, (_, dil) in enumerate(ATTN_GROUPS):
        @pl.when(is_attn & (lax.rem(ja, ng) == g))
        def _():
            _attn_tile(u_ref, w_ref, cos_ref, sin_ref, qkv[g], scr, scr2, dil=dil, sub=sub)

    for src, dst in zip(side_in, side_out):
        dst[...] = src[...].astype(dst.dtype)


def _in_proj(u, w, side_weights, tm=SEQ, tn=512, sub=1024):
    m, d = u.shape
    ng = len(ATTN_GROUPS)
    assert tn == ATTN_HEADS_PER_GROUP * HEAD_DIM and all(dl in (1, 4, 16) for _, dl in ATTN_GROUPS)
    attn_lo = 4 * HG_WIDTH // tn
    n_attn = 3 * ng
    nj = w.shape[1] // tn
    n_main = nj - n_attn
    steps = (m // tm) * nj

    half = HEAD_DIM // 2
    inv_freq = jnp.exp(-math.log(ROPE_THETA) * jnp.arange(half, dtype=F32) * (2.0 / HEAD_DIM))
    ang = jnp.arange(SEQ, dtype=F32)[:, None] * inv_freq[None, :]
    cos, sin = jnp.cos(ang), jnp.sin(ang)
    one, zero = jnp.ones((SEQ, HEAD_DIM), F32), jnp.zeros((SEQ, HEAD_DIM), F32)
    cos_t = jnp.stack([jnp.concatenate([cos, cos], axis=1), one])
    sin_t = jnp.stack([jnp.concatenate([-sin, sin], axis=1), zero])

    def part(j):
        return jnp.clip((j - attn_lo) // ng, 0, 2)

    table = pl.BlockSpec((None, SEQ, HEAD_DIM), lambda i, j: (part(j) // 2, 0, 0))

    def main_col(j):
        return jnp.where(j < attn_lo, j, jnp.maximum(j - n_attn, attn_lo - 1))

    def qkv_spec(g):
        return pl.BlockSpec((tm, tn), lambda i, j: (i, jnp.clip((j - attn_lo - g) // ng, 0, 2)))

    side_specs, side_shape = [], []
    for sw in side_weights:
        rows = -(-sw.shape[0] // steps)
        rows = -(-rows // BF16_SUBLANES) * BF16_SUBLANES
        while sw.shape[0] % rows:
            rows += BF16_SUBLANES
        last = sw.shape[0] // rows - 1
        side_specs.append(pl.BlockSpec((rows, sw.shape[1]), lambda i, j, last=last: (jnp.minimum(i * nj + j, last), 0)))
        side_shape.append(jax.ShapeDtypeStruct(sw.shape, BF16))
    heads = tn // HEAD_DIM
    outs = pl.pallas_call(
        functools.partial(_inproj_kernel, n_side=len(side_weights), attn_lo=attn_lo, sub=sub),
        grid=(m // tm, nj),
        in_specs=[
            pl.BlockSpec((tm, d), lambda i, j: (i, 0), pipeline_mode=pl.Buffered(1)),
            pl.BlockSpec((d, tn), lambda i, j: (0, j)),
            table, table,
            *side_specs,
        ],
        out_specs=[pl.BlockSpec((tm, tn), lambda i, j: (i, main_col(j))),
                   *[qkv_spec(g) for g in range(ng)], *side_specs],
        out_shape=[jax.ShapeDtypeStruct((m, n_main * tn), F32),
                   *[jax.ShapeDtypeStruct((m, 3 * tn), BF16)] * ng, *side_shape],
        scratch_shapes=[pltpu.VMEM((heads, sub, HEAD_DIM), F32), pltpu.VMEM((heads, 4, sub // 4, HEAD_DIM), F32)],
        compiler_params=_params("arbitrary", "arbitrary"),
        name="in_proj",
    )(u, w, cos_t, sin_t, *side_weights)
    return outs[0], outs[1:1 + ng], outs[1 + ng:]


HG_CHUNK = 128
HG_HEADS_PER_STEP = 4
HG_CHUNKS_PER_STEP = 8


def _hgrn_tables(c):
    t = np.arange(c)[:, None]
    s = np.arange(c)[None, :]
    x = t ^ s
    lvl = np.where(x > 0, np.floor(np.log2(np.maximum(x, 1))), -1).astype(np.int32)
    lvl = np.where(s > t, -2, lvl).astype(np.int32)
    tri = (s <= t).astype(np.float32)
    return jnp.asarray(tri, BF16), jnp.asarray(lvl, jnp.int32)


def _neg_abs(d):
    bits = lax.bitcast_convert_type(d, jnp.uint32) | jnp.uint32(0x80000000)
    return lax.bitcast_convert_type(bits, F32)


def _hgrn_chunk(zq, zf, vv, zg, sts, lb, oml, ng, tri, lvl, row):
    heads = range(len(zq))
    c = zq[0].shape[0]
    f = [lb[i] + oml[i] * _sigmoid(zf[i]) for i in heads]
    kk = [1.0 - f[i] for i in heads]
    qq = [zq[i] * _sigmoid(zq[i]) for i in heads]
    gl = [jnp.log(f[i]) * LOG2E for i in heads]

    gcum = []
    for i in heads:
        g1 = gl[i].astype(BF16)
        g2 = (gl[i] - g1.astype(F32)).astype(BF16)
        gc = _dot(tri, jnp.concatenate([g1, g2], axis=1))
        gcum.append(gc[:, :LANES] + gc[:, LANES:])
    glast = [gcum[i][c - 1:c, :] for i in heads]

    qb = [qq[i].astype(BF16) for i in heads]
    kb = [kk[i].astype(BF16) for i in heads]
    st_in, a = list(sts), []
    for i in heads:
        kd = (kk[i] * jnp.exp2(glast[i] - gcum[i])).astype(BF16)
        st_in.append(st_in[i] * jnp.exp2(glast[i]) + _dot_tn(vv[i], kd))
        a.append(jnp.where(lvl == -1, _dot_nt(qb[i], kb[i]), 0.0))

    h = c // 2
    while h >= 1:
        in_level = lvl == int(math.log2(h))
        for i in heads:
            if h >= 4:
                g3d = gcum[i].reshape(c // (2 * h), 2 * h, LANES)
                ref = jnp.broadcast_to(g3d[:, h - 1:h, :], g3d.shape).reshape(c, LANES)
                x = _neg_abs(gcum[i] - ref)
            elif h == 2:
                pos = row & 3
                up = pltpu.roll(gl[i], c - 1, axis=0)
                dn = pltpu.roll(gl[i], 1, axis=0)
                x = jnp.where(pos == 0, up, jnp.where(pos == 1, 0.0, jnp.where(pos == 2, gl[i], gl[i] + dn)))
            else:
                x = jnp.where((row & 1) == 1, gl[i], 0.0)
            e = jnp.exp2(x).astype(BF16)
            a[i] = jnp.where(in_level, _dot_nt(qb[i] * e, kb[i] * e), a[i])
        h //= 2
    o = []
    for i in heads:
        qg = (qq[i] * jnp.exp2(gcum[i])).astype(BF16)
        o.append(_dot_nt(qg, st_in[i].astype(BF16)) + _dot(a[i].astype(BF16), vv[i]))
    y = [_rmsnorm_rows(o[i], ng) * (zg[i] * _sigmoid(zg[i])) for i in heads]
    return y, st_in[len(zq):]


def _hgrn_kernel(q_ref, f_ref, i_ref, g_ref, lbp_ref, ng_ref, tri_ref, lvl_ref, o_ref, *, chunk, heads):
    c = chunk
    seq = q_ref.shape[0]
    p = lbp_ref[...]
    pe = jnp.exp(p - jnp.max(p, axis=0, keepdims=True))
    lb = pe[0:1, :] / jnp.sum(pe, axis=0, keepdims=True)
    oml = 1.0 - lb
    ng = ng_ref[...]
    row = lax.broadcasted_iota(jnp.int32, (c, LANES), 0)

    lanes = [slice(hh * LANES, (hh + 1) * LANES) for hh in range(heads)]

    nch = HG_CHUNKS_PER_STEP

    def body(ci, sts):
        ent = [(pl.ds(pl.multiple_of((ci * nch + ch) * c, c), c), ln) for ch in range(nch) for ln in lanes]
        y, new = _hgrn_chunk(
            [q_ref[rows, ln] for rows, ln in ent], [f_ref[rows, ln] for rows, ln in ent],
            [i_ref[rows, ln].astype(BF16) for rows, ln in ent], [g_ref[rows, ln] for rows, ln in ent],
            sts, [lb[:, ln] for _, ln in ent], [oml[:, ln] for _, ln in ent],
            ng, tri_ref[...], lvl_ref[...], row)
        for (rows, ln), yy in zip(ent, y):
            o_ref[rows, ln] = yy.astype(o_ref.dtype)
        return tuple(new)

    lax.fori_loop(0, seq // (c * nch), body, tuple(jnp.zeros((HG_DIM, HG_DIM), F32) for _ in range(heads)))


def _hgrn2(proj, lb_param, norm_g, batch):
    c, hp = HG_CHUNK, HG_HEADS_PER_STEP
    tri, lvl = _hgrn_tables(c)
    m = proj.shape[0]
    w = hp * LANES

    def col(off):
        return pl.BlockSpec((SEQ, w), lambda b, h, o=off // hp: (b, o + h))

    return pl.pallas_call(
        functools.partial(_hgrn_kernel, chunk=c, heads=hp),
        grid=(batch, HG_HEADS // hp),
        in_specs=[
            col(_OFF_HQ), col(_OFF_HF), col(_OFF_HI), col(_OFF_HG),
            pl.BlockSpec((lb_param.shape[0], w), lambda b, h: (0, h)),
            pl.BlockSpec((1, LANES), lambda b, h: (0, 0)),
            pl.BlockSpec((c, c), lambda b, h: (0, 0)),
            pl.BlockSpec((c, c), lambda b, h: (0, 0)),
        ],
        out_specs=pl.BlockSpec((SEQ, w), lambda b, h: (b, h)),
        out_shape=jax.ShapeDtypeStruct((m, HG_WIDTH), BF16),
        compiler_params=_params("parallel", "parallel"),
        name="hgrn2",
    )(proj, proj, proj, proj, lb_param, norm_g.reshape(1, LANES), tri, lvl)


ATTN_BLOCK = 128
ATTN_BATCH = 8


def _attn_kernel(q0, k0, v0, q1, k1, v1, q2, k2, v2, o_ref, og_ref, lse_ref):
    blk = ATTN_BLOCK
    scale = HEAD_DIM ** -0.5
    qi = lax.broadcasted_iota(jnp.int32, (blk, 2 * blk), 0)
    ki = lax.broadcasted_iota(jnp.int32, (blk, 2 * blk), 1)
    mask2 = (ki >= qi) & (ki <= qi + blk)
    mask1 = (lax.broadcasted_iota(jnp.int32, (blk, blk), 1)
             <= lax.broadcasted_iota(jnp.int32, (blk, blk), 0))

    def run_batch(items):
        s = [_dot_nt(qj, kw) * (scale * LOG2E) for qj, kw, _, _, _ in items]
        ps, mxs = [], []
        for (_, _, _, mask, _), sj in zip(items, s):
            sj = jnp.where(mask, sj, NEG_INF)
            mx = jnp.max(sj, axis=-1, keepdims=True)
            ps.append(jnp.exp2(sj - mx).astype(BF16))
            mxs.append(mx)
        os = [_dot(p, jnp.concatenate([vw, jnp.ones_like(vw)], axis=1)) for p, (_, _, vw, _, _) in zip(ps, items)]
        for (_, _, _, _, (g, rows)), o, mx in zip(items, os, mxs):
            den = o[:, HEAD_DIM:]
            og_ref.at[g][rows, :] = o[:, :HEAD_DIM] * (1.0 / den)
            lse_ref.at[g][rows, :] = mx + jnp.log(den) * LOG2E

    refs = ((q0, k0, v0), (q1, k1, v1), (q2, k2, v2))
    pending = []
    for g, (window, dil) in enumerate(ATTN_GROUPS):
        assert window // dil == blk
        q_ref, k_ref, v_ref = refs[g]
        length = SEQ // dil
        for r in range(dil):
            base = (_attn_slot(r) if dil == 16 else r) * length
            for j in range(length // blk):
                lo, hi = base + max(j - 1, 0) * blk, base + (j + 1) * blk
                rows = pl.ds(j * blk, blk) if dil == 1 else pl.ds(j * blk * dil + r, blk, stride=dil)
                pending.append((q_ref[hi - blk:hi, :], k_ref[lo:hi, :], v_ref[lo:hi, :],
                                mask1 if j == 0 else mask2, (g, rows)))
                if len(pending) == ATTN_BATCH:
                    run_batch(pending)
                    pending = []
    assert not pending

    l0, l1, l2 = lse_ref[0], lse_ref[1], lse_ref[2]
    mx = jnp.maximum(jnp.maximum(l0, l1), l2)
    w0, w1, w2 = jnp.exp2(l0 - mx), jnp.exp2(l1 - mx), jnp.exp2(l2 - mx)
    o = (w0 * og_ref[0] + w1 * og_ref[1] + w2 * og_ref[2]) / (w0 + w1 + w2)
    o_ref[...] = o.astype(o_ref.dtype)


def _attention(qkv, batch):
    m = qkv[0].shape[0]
    hg = ATTN_HEADS_PER_GROUP

    def col(part):
        return pl.BlockSpec((SEQ, LANES), lambda b, h, o=part * hg: (b, o + h))

    return pl.pallas_call(
        _attn_kernel,
        grid=(batch, hg),
        in_specs=[col(0), col(1), col(2)] * len(ATTN_GROUPS),
        out_specs=pl.BlockSpec((SEQ, LANES), lambda b, h: (b, h)),
        out_shape=jax.ShapeDtypeStruct((m, ATTN_OUT), BF16),
        scratch_shapes=[pltpu.VMEM((len(ATTN_GROUPS), SEQ, LANES), F32)] * 2,
        compiler_params=_params("parallel", "parallel"),
        name="dilated_attn",
    )(*[a for g in qkv for a in (g, g, g)])


MIX_GATE_COLS = 512


def _mix_kernel(*refs):
    nc = D_MODEL // MIX_GATE_COLS
    oa_ref, ob_ref = refs[0], refs[1]
    ga_refs, gb_refs = refs[2:2 + nc], refs[2 + nc:2 + 2 * nc]
    x_ref, wpa_ref, wpb_ref, wo_ref, n2_ref, h_ref, v_ref = refs[2 + 2 * nc:]
    oa, ob = oa_ref[...], ob_ref[...]
    parts = []
    for c in range(nc):
        cols = slice(c * MIX_GATE_COLS, (c + 1) * MIX_GATE_COLS)
        ya = _dot(oa, wpa_ref[:, cols])
        yb = _dot(ob, wpb_ref[:, cols])
        mix = jax.nn.sigmoid(ga_refs[c][...]) * ya + jax.nn.sigmoid(gb_refs[c][...]) * yb
        parts.append(mix.astype(BF16))
    h = x_ref[...] + _dot(jnp.concatenate(parts, axis=1), wo_ref[...])
    h_ref[...] = h
    v_ref[...] = _rmsnorm_rows(h, n2_ref[...]).astype(v_ref.dtype)


def _mix(oa, ob, proj, x2, wpa, wpb, wo, norm2_g, tm=256):
    m, d = x2.shape
    gc = MIX_GATE_COLS
    nc = d // gc
    resident = dict(pipeline_mode=pl.Buffered(1))

    def gate(off, c):
        return pl.BlockSpec((tm, gc), lambda i, o=off * LANES // gc + c: (i, o))

    return pl.pallas_call(
        _mix_kernel,
        grid=(m // tm,),
        in_specs=[
            pl.BlockSpec((tm, HG_WIDTH), lambda i: (i, 0)),
            pl.BlockSpec((tm, ATTN_OUT), lambda i: (i, 0)),
            *[gate(_OFF_GA, c) for c in range(nc)],
            *[gate(_OFF_GB, c) for c in range(nc)],
            pl.BlockSpec((tm, d), lambda i: (i, 0)),
            pl.BlockSpec(wpa.shape, lambda i: (0, 0), **resident),
            pl.BlockSpec(wpb.shape, lambda i: (0, 0), **resident),
            pl.BlockSpec(wo.shape, lambda i: (0, 0), **resident),
            pl.BlockSpec((1, d), lambda i: (0, 0)),
        ],
        out_specs=[pl.BlockSpec((tm, d), lambda i: (i, 0)), pl.BlockSpec((tm, d), lambda i: (i, 0))],
        out_shape=[jax.ShapeDtypeStruct((m, d), F32), jax.ShapeDtypeStruct((m, d), BF16)],
        compiler_params=_params("parallel"),
        name="mix_outproj",
    )(oa, ob, *([proj] * (2 * nc)), x2, wpa, wpb, wo, norm2_g.reshape(1, d))


def _shift_rows(h, prev, k):
    r = pltpu.roll(h, k, axis=0)
    p = pltpu.roll(prev, k, axis=0)
    row8 = lax.broadcasted_iota(jnp.int32, prev.shape, 0)
    top = jnp.where(row8 < k, p, r[:8])
    return jnp.concatenate([top, r[8:]], axis=0)


def _causal_conv3(h, prev, w, b):
    return w[2:3] * h + w[1:2] * _shift_rows(h, prev, 1) + w[0:1] * _shift_rows(h, prev, 2) + b


def _gelu_tanh_gate(x, y):
    c = math.sqrt(2.0 / math.pi)
    w = x * ((-2.0 * c * LOG2E) + (-2.0 * c * 0.044715 * LOG2E) * (x * x))
    return (x * y) * (1.0 / (1.0 + jnp.exp2(w)))


def _ffn_up_kernel(v_ref, halo_ref, wa_ref, wb_ref, cwa_ref, cwb_ref, cba_ref, cbb_ref, o_ref, *, sub):
    tm = v_ref.shape[0]
    hr = halo_ref.shape[0]
    starts_sequence = (pl.program_id(0) * tm) % SEQ == 0
    halo = jnp.where(starts_sequence, jnp.zeros_like(halo_ref[...]), halo_ref[...])
    wa, wb = wa_ref[...].astype(BF16), wb_ref[...].astype(BF16)
    cwa, cwb, cba, cbb = cwa_ref[...], cwb_ref[...], cba_ref[...], cbb_ref[...]
    pa = pb = None
    for r0 in range(0, tm, sub):
        vs = v_ref[r0:r0 + sub, :]
        if r0 == 0:
            vs = jnp.concatenate([halo, vs], axis=0)
        ha, hb = _dot(vs, wa), _dot(vs, wb)
        if r0 == 0:
            pa, pb = ha[hr - 8:hr], hb[hr - 8:hr]
            ha, hb = ha[hr:], hb[hr:]
        ca = _causal_conv3(ha, pa, cwa, cba)
        cb = _causal_conv3(hb, pb, cwb, cbb)
        o_ref[r0:r0 + sub, :] = _gelu_tanh_gate(ca, cb).astype(o_ref.dtype)
        pa, pb = ha[sub - 8:], hb[sub - 8:]


def _ffn_up(v, w_up, conv_w, conv_b, tm=2048, tn=512, sub=1024):
    m, d = v.shape
    nj = D_FF // tn
    hb = tm // BF16_SUBLANES
    return pl.pallas_call(
        functools.partial(_ffn_up_kernel, sub=sub),
        grid=(m // tm, nj),
        in_specs=[
            pl.BlockSpec((tm, d), lambda i, j: (i, 0)),
            pl.BlockSpec((BF16_SUBLANES, d), lambda i, j: (jnp.maximum(i * hb - 1, 0), 0)),
            pl.BlockSpec((d, tn), lambda i, j: (0, j)),
            pl.BlockSpec((d, tn), lambda i, j: (0, j + nj)),
            pl.BlockSpec((3, tn), lambda i, j: (0, j)),
            pl.BlockSpec((3, tn), lambda i, j: (0, j + nj)),
            pl.BlockSpec((1, tn), lambda i, j: (0, j)),
            pl.BlockSpec((1, tn), lambda i, j: (0, j + nj)),
        ],
        out_specs=pl.BlockSpec((tm, tn), lambda i, j: (i, j)),
        out_shape=jax.ShapeDtypeStruct((m, D_FF), BF16),
        compiler_params=_params("parallel", "parallel"),
        name="ffn_up_conv_gate",
    )(v, v, w_up, w_up, conv_w, conv_w, conv_b.reshape(1, -1), conv_b.reshape(1, -1))


def _ffn_down_kernel(a_ref, w_ref, h_ref, g_ref, o_ref):
    o_ref[...] = _rmsnorm_rows(h_ref[...] + _dot(a_ref[...], w_ref[...]), g_ref[...])


def _ffn_down(act, w_down, h1, norm_g, tm=256):
    m, kdim = act.shape
    d = w_down.shape[1]
    return pl.pallas_call(
        _ffn_down_kernel,
        grid=(m // tm,),
        in_specs=[
            pl.BlockSpec((tm, kdim), lambda i: (i, 0)),
            pl.BlockSpec((kdim, d), lambda i: (0, 0), pipeline_mode=pl.Buffered(1)),
            pl.BlockSpec((tm, d), lambda i: (i, 0)),
            pl.BlockSpec((1, d), lambda i: (0, 0)),
        ],
        out_specs=pl.BlockSpec((tm, d), lambda i: (i, 0)),
        out_shape=jax.ShapeDtypeStruct((m, d), F32),
        compiler_params=_params("parallel"),
        name="ffn_down_norm",
    )(act, w_down, h1, norm_g.reshape(1, d))


def kernel(x, norm1_g, w_in, hg_lb_param, hg_norm_g, w_proj_a, w_proj_b, w_out,
           norm2_g, w_up, conv_w, conv_b, w_down, norm_f_g):
    batch, seq, d = x.shape
    depth = w_in.shape[0]
    assert (seq, d, depth) == (SEQ, D_MODEL, 1) and w_in.shape[2] == IN_COLS
    h = x.reshape(batch * seq, d)
    for l in range(depth):
        u = _norm_bf16(h, norm1_g[l])
        proj, qkv, (wpa, wpb, wo, wdn) = _in_proj(u, w_in[l], (w_proj_a[l], w_proj_b[l], w_out[l], w_down[l]))
        oa = _hgrn2(proj, hg_lb_param, hg_norm_g[l], batch)
        ob = _attention(qkv, batch)
        h, v = _mix(oa, ob, proj, h, wpa, wpb, wo, norm2_g[l])
        act = _ffn_up(v, w_up[l], conv_w[l], conv_b[l])
        out = _ffn_down(act, wdn, h, norm_f_g)
    return out.reshape(batch, seq, d)
```

```python
import functools
import math

import numpy as np
import jax
import jax.numpy as jnp
from jax import lax
from jax.experimental import pallas as pl
from jax.experimental.pallas import tpu as pltpu

D_MODEL = 2048
SEQ = 2048
HG_HEADS = 16
HG_DIM = 128
HG_WIDTH = HG_HEADS * HG_DIM
ATTN_GROUPS = ((128, 1), (512, 4), (2048, 16))
ATTN_HEADS_PER_GROUP = 4
HEAD_DIM = 128
ATTN_WIDTH = len(ATTN_GROUPS) * ATTN_HEADS_PER_GROUP * HEAD_DIM
ATTN_OUT = ATTN_HEADS_PER_GROUP * HEAD_DIM
ROPE_THETA = 10000.0
D_FF = 5632
NORM_EPS = 1e-6
NEG_INF = -1e30
IN_COLS = 4 * HG_WIDTH + 3 * ATTN_WIDTH + 2 * D_MODEL

_OFF_HQ = 0
_OFF_HF = HG_WIDTH // 128
_OFF_HI = 2 * HG_WIDTH // 128
_OFF_HG = 3 * HG_WIDTH // 128
_OFF_AQ = 4 * HG_WIDTH // 128
_OFF_AK = _OFF_AQ + ATTN_WIDTH // 128
_OFF_AV = _OFF_AK + ATTN_WIDTH // 128
_OFF_GA = 4 * HG_WIDTH // 128
_OFF_GB = _OFF_GA + D_MODEL // 128

LANES = 128
BF16_SUBLANES = 16
VMEM_LIMIT = 56 * 1024 * 1024
LOG2E = 1.4426950408889634

F32 = jnp.float32
BF16 = jnp.bfloat16


def _dot(a, b):
    return jnp.dot(a, b, preferred_element_type=F32)


def _dot_nt(a, b):
    return lax.dot_general(a, b, (((1,), (1,)), ((), ())), preferred_element_type=F32)


def _dot_tn(a, b):
    return lax.dot_general(a, b, (((0,), (0,)), ((), ())), preferred_element_type=F32)


def _rmsnorm_rows(x, g):
    ms = jnp.mean(x * x, axis=-1, keepdims=True)
    return x * lax.rsqrt(ms + NORM_EPS) * g


def _sigmoid(x):
    return 1.0 / (1.0 + jnp.exp2(x * (-LOG2E)))


def _params(*sem):
    return pltpu.CompilerParams(dimension_semantics=sem, vmem_limit_bytes=VMEM_LIMIT)


def _norm_kernel(x_ref, g_ref, u_ref):
    u_ref[...] = _rmsnorm_rows(x_ref[...], g_ref[...]).astype(u_ref.dtype)


def _norm_bf16(x2, norm_g, tm=512):
    m, d = x2.shape
    return pl.pallas_call(
        _norm_kernel,
        grid=(m // tm,),
        in_specs=[pl.BlockSpec((tm, d), lambda i: (i, 0)), pl.BlockSpec((1, d), lambda i: (0, 0))],
        out_specs=pl.BlockSpec((tm, d), lambda i: (i, 0)),
        out_shape=jax.ShapeDtypeStruct((m, d), BF16),
        compiler_params=_params("parallel"),
        name="norm1",
    )(x2, norm_g.reshape(1, d))


def _inproj_kernel(u_ref, w_ref, *refs):
    n_side = (len(refs) - 1) // 2
    o_ref = refs[n_side]
    o_ref[...] = _dot(u_ref[...], w_ref[...].astype(BF16))
    for src, dst in zip(refs[:n_side], refs[n_side + 1:]):
        dst[...] = src[...].astype(dst.dtype)


def _in_proj(u, w, side_weights, tm=2048, tn=512):
    m, d = u.shape
    n_hg, skip = 4 * HG_WIDTH // tn, 3 * ATTN_WIDTH // tn
    nj = n_hg + 2 * D_MODEL // tn
    steps = (m // tm) * nj
    side_in, side_out, side_shape = [], [], []
    for sw in side_weights:
        rows = -(-sw.shape[0] // steps)
        rows = -(-rows // BF16_SUBLANES) * BF16_SUBLANES
        assert sw.shape[0] % rows == 0
        last = sw.shape[0] // rows - 1
        spec = pl.BlockSpec((rows, sw.shape[1]), lambda i, j, last=last: (jnp.minimum(i * nj + j, last), 0))
        side_in.append(spec)
        side_out.append(spec)
        side_shape.append(jax.ShapeDtypeStruct(sw.shape, BF16))
    outs = pl.pallas_call(
        _inproj_kernel,
        grid=(m // tm, nj),
        in_specs=[
            pl.BlockSpec((tm, d), lambda i, j: (i, 0)),
            pl.BlockSpec((d, tn), lambda i, j: (0, jnp.where(j < n_hg, j, j + skip))),
            *side_in,
        ],
        out_specs=[pl.BlockSpec((tm, tn), lambda i, j: (i, j)), *side_out],
        out_shape=[jax.ShapeDtypeStruct((m, nj * tn), F32), *side_shape],
        compiler_params=_params("arbitrary", "arbitrary"),
        name="in_proj",
    )(u, w, *side_weights)
    return outs[0], outs[1:]


def _attn_proj_kernel(u_ref, w_ref, cos_ref, sin_ref, o_ref, scr, scr2, *, dil, sub):
    w = w_ref[...].astype(BF16)
    tm = u_ref.shape[0]
    length, n = tm // dil, sub // dil
    for r0 in range(0, tm, sub):
        acc = _dot(u_ref[r0:r0 + sub, :], w)
        cs, sn = cos_ref[r0:r0 + sub, :], sin_ref[r0:r0 + sub, :]
        for hh in range(acc.shape[1] // HEAD_DIM):
            ln = slice(hh * HEAD_DIM, (hh + 1) * HEAD_DIM)
            x = acc[:, ln]
            y = x * cs + pltpu.roll(x, HEAD_DIM // 2, axis=1) * sn
            if dil == 1:
                o_ref[r0:r0 + sub, ln] = y.astype(o_ref.dtype)
                continue
            scr[hh] = y
            for r1 in range(4):
                part = scr[hh, pl.ds(r1, sub // 4, stride=4), :]
                if dil == 4:
                    dst = r1 * length + r0 // dil
                    o_ref[dst:dst + n, ln] = part.astype(o_ref.dtype)
                    continue
                scr2[hh, r1] = part
                for r2 in range(4):
                    dst = _attn_slot(r1 + 4 * r2) * length + r0 // dil
                    o_ref[dst:dst + n, ln] = scr2[hh, r1, pl.ds(r2, n, stride=4), :].astype(o_ref.dtype)


def _attn_slot(r):
    return (r % 4) * 4 + r // 4


def _attn_proj(u, w, group, tm=SEQ, sub=512):
    m, d = u.shape
    dil = ATTN_GROUPS[group][1]
    assert dil in (1, 4, 16)
    tn = ATTN_HEADS_PER_GROUP * HEAD_DIM
    half = HEAD_DIM // 2
    inv_freq = jnp.exp(-math.log(ROPE_THETA) * jnp.arange(half, dtype=F32) * (2.0 / HEAD_DIM))
    ang = jnp.arange(SEQ, dtype=F32)[:, None] * inv_freq[None, :]
    cos, sin = jnp.cos(ang), jnp.sin(ang)
    one, zero = jnp.ones((SEQ, HEAD_DIM), F32), jnp.zeros((SEQ, HEAD_DIM), F32)
    cos_t = jnp.stack([jnp.concatenate([cos, cos], axis=1), one])
    sin_t = jnp.stack([jnp.concatenate([-sin, sin], axis=1), zero])
    first = (_OFF_AQ * LANES) // tn + group
    per = ATTN_WIDTH // tn
    table = pl.BlockSpec((None, SEQ, HEAD_DIM), lambda i, j: (j // 2, 0, 0))
    return pl.pallas_call(
        functools.partial(_attn_proj_kernel, dil=dil, sub=sub),
        grid=(m // tm, 3),
        in_specs=[
            pl.BlockSpec((tm, d), lambda i, j: (i, 0)),
            pl.BlockSpec((d, tn), lambda i, j: (0, first + per * j)),
            table, table,
        ],
        out_specs=pl.BlockSpec((tm, tn), lambda i, j: (i, j)),
        out_shape=jax.ShapeDtypeStruct((m, 3 * tn), BF16),
        scratch_shapes=[pltpu.VMEM((tn // HEAD_DIM, sub, HEAD_DIM), F32),
                        pltpu.VMEM((tn // HEAD_DIM, 4, sub // 4, HEAD_DIM), F32)],
        compiler_params=_params("parallel", "parallel"),
        name="attn_proj",
    )(u, w, cos_t, sin_t)


HG_CHUNK = 128
HG_HEADS_PER_STEP = 4
HG_CHUNKS_PER_STEP = 4


def _hgrn_tables(c):
    t = np.arange(c)[:, None]
    s = np.arange(c)[None, :]
    x = t ^ s
    lvl = np.where(x > 0, np.floor(np.log2(np.maximum(x, 1))), -1).astype(np.int32)
    lvl = np.where(s > t, -2, lvl).astype(np.int32)
    tri = (s <= t).astype(np.float32)
    return jnp.asarray(tri, BF16), jnp.asarray(lvl, jnp.int32)


def _neg_abs(d):
    bits = lax.bitcast_convert_type(d, jnp.uint32) | jnp.uint32(0x80000000)
    return lax.bitcast_convert_type(bits, F32)


def _hgrn_chunk(zq, zf, vv, zg, sts, lb, oml, ng, tri, lvl, row):
    heads = range(len(zq))
    c = zq[0].shape[0]
    f = [lb[i] + oml[i] * _sigmoid(zf[i]) for i in heads]
    kk = [1.0 - f[i] for i in heads]
    qq = [zq[i] * _sigmoid(zq[i]) for i in heads]
    gl = [jnp.log(f[i]) * LOG2E for i in heads]

    gcum = []
    for i in heads:
        g1 = gl[i].astype(BF16)
        g2 = (gl[i] - g1.astype(F32)).astype(BF16)
        gc = _dot(tri, jnp.concatenate([g1, g2], axis=1))
        gcum.append(gc[:, :LANES] + gc[:, LANES:])
    glast = [gcum[i][c - 1:c, :] for i in heads]

    qb = [qq[i].astype(BF16) for i in heads]
    kb = [kk[i].astype(BF16) for i in heads]
    st_in, a = list(sts), []
    for i in heads:
        kd = (kk[i] * jnp.exp2(glast[i] - gcum[i])).astype(BF16)
        st_in.append(st_in[i] * jnp.exp2(glast[i]) + _dot_tn(vv[i], kd))
        a.append(jnp.where(lvl == -1, _dot_nt(qb[i], kb[i]), 0.0))

    h = c // 2
    while h >= 1:
        in_level = lvl == int(math.log2(h))
        for i in heads:
            if h >= 4:
                g3d = gcum[i].reshape(c // (2 * h), 2 * h, LANES)
                ref = jnp.broadcast_to(g3d[:, h - 1:h, :], g3d.shape).reshape(c, LANES)
                x = _neg_abs(gcum[i] - ref)
            elif h == 2:
                pos = row & 3
                up = pltpu.roll(gl[i], c - 1, axis=0)
                dn = pltpu.roll(gl[i], 1, axis=0)
                x = jnp.where(pos == 0, up, jnp.where(pos == 1, 0.0, jnp.where(pos == 2, gl[i], gl[i] + dn)))
            else:
                x = jnp.where((row & 1) == 1, gl[i], 0.0)
            e = jnp.exp2(x).astype(BF16)
            a[i] = jnp.where(in_level, _dot_nt(qb[i] * e, kb[i] * e), a[i])
        h //= 2
    o = []
    for i in heads:
        qg = (qq[i] * jnp.exp2(gcum[i])).astype(BF16)
        o.append(_dot_nt(qg, st_in[i].astype(BF16)) + _dot(a[i].astype(BF16), vv[i]))
    y = [_rmsnorm_rows(o[i], ng) * (zg[i] * _sigmoid(zg[i])) for i in heads]
    return y, st_in[len(zq):]


def _hgrn_kernel(q_ref, f_ref, i_ref, g_ref, lbp_ref, ng_ref, tri_ref, lvl_ref, o_ref, *, chunk, heads):
    c = chunk
    seq = q_ref.shape[0]
    p = lbp_ref[...]
    pe = jnp.exp(p - jnp.max(p, axis=0, keepdims=True))
    lb = pe[0:1, :] / jnp.sum(pe, axis=0, keepdims=True)
    oml = 1.0 - lb
    ng = ng_ref[...]
    row = lax.broadcasted_iota(jnp.int32, (c, LANES), 0)
    lanes = [slice(hh * LANES, (hh + 1) * LANES) for hh in range(heads)]
    nch = HG_CHUNKS_PER_STEP

    def body(ci, sts):
        ent = [(pl.ds(pl.multiple_of((ci * nch + ch) * c, c), c), ln) for ch in range(nch) for ln in lanes]
        y, new = _hgrn_chunk(
            [q_ref[rows, ln] for rows, ln in ent], [f_ref[rows, ln] for rows, ln in ent],
            [i_ref[rows, ln].astype(BF16) for rows, ln in ent], [g_ref[rows, ln] for rows, ln in ent],
            sts, [lb[:, ln] for _, ln in ent], [oml[:, ln] for _, ln in ent],
            ng, tri_ref[...], lvl_ref[...], row)
        for (rows, ln), yy in zip(ent, y):
            o_ref[rows, ln] = yy.astype(o_ref.dtype)
        return tuple(new)

    lax.fori_loop(0, seq // (c * nch), body, tuple(jnp.zeros((HG_DIM, HG_DIM), F32) for _ in range(heads)))


def _hgrn2(proj, lb_param, norm_g, batch):
    c, hp = HG_CHUNK, HG_HEADS_PER_STEP
    tri, lvl = _hgrn_tables(c)
    m = proj.shape[0]
    w = hp * LANES

    def col(off):
        return pl.BlockSpec((SEQ, w), lambda b, h, o=off // hp: (b, o + h))

    return pl.pallas_call(
        functools.partial(_hgrn_kernel, chunk=c, heads=hp),
        grid=(batch, HG_HEADS // hp),
        in_specs=[
            col(_OFF_HQ), col(_OFF_HF), col(_OFF_HI), col(_OFF_HG),
            pl.BlockSpec((lb_param.shape[0], w), lambda b, h: (0, h)),
            pl.BlockSpec((1, LANES), lambda b, h: (0, 0)),
            pl.BlockSpec((c, c), lambda b, h: (0, 0)),
            pl.BlockSpec((c, c), lambda b, h: (0, 0)),
        ],
        out_specs=pl.BlockSpec((SEQ, w), lambda b, h: (b, h)),
        out_shape=jax.ShapeDtypeStruct((m, HG_WIDTH), BF16),
        compiler_params=_params("parallel", "parallel"),
        name="hgrn2",
    )(proj, proj, proj, proj, lb_param, norm_g.reshape(1, LANES), tri, lvl)


ATTN_BLOCK = 128
ATTN_BATCH = 8


def _attn_kernel(q0, k0, v0, q1, k1, v1, q2, k2, v2, o_ref, og_ref, lse_ref):
    blk = ATTN_BLOCK
    scale = HEAD_DIM ** -0.5
    qi = lax.broadcasted_iota(jnp.int32, (blk, 2 * blk), 0)
    ki = lax.broadcasted_iota(jnp.int32, (blk, 2 * blk), 1)
    mask2 = (ki >= qi) & (ki <= qi + blk)
    mask1 = (lax.broadcasted_iota(jnp.int32, (blk, blk), 1)
             <= lax.broadcasted_iota(jnp.int32, (blk, blk), 0))

    def run_batch(items):
        s = [_dot_nt(qj, kw) * (scale * LOG2E) for qj, kw, _, _, _ in items]
        ps, mxs = [], []
        for (_, _, _, mask, _), sj in zip(items, s):
            sj = jnp.where(mask, sj, NEG_INF)
            mx = jnp.max(sj, axis=-1, keepdims=True)
            ps.append(jnp.exp2(sj - mx).astype(BF16))
            mxs.append(mx)
        os = [_dot(p, jnp.concatenate([vw, jnp.ones_like(vw)], axis=1)) for p, (_, _, vw, _, _) in zip(ps, items)]
        for (_, _, _, _, (g, rows)), o, mx in zip(items, os, mxs):
            den = o[:, HEAD_DIM:]
            og_ref.at[g][rows, :] = o[:, :HEAD_DIM] * (1.0 / den)
            lse_ref.at[g][rows, :] = mx + jnp.log(den) * LOG2E

    refs = ((q0, k0, v0), (q1, k1, v1), (q2, k2, v2))
    pending = []
    for g, (window, dil) in enumerate(ATTN_GROUPS):
        assert window // dil == blk
        q_ref, k_ref, v_ref = refs[g]
        length = SEQ // dil
        for r in range(dil):
            base = (_attn_slot(r) if dil == 16 else r) * length
            for j in range(length // blk):
                lo, hi = base + max(j - 1, 0) * blk, base + (j + 1) * blk
                rows = pl.ds(j * blk, blk) if dil == 1 else pl.ds(j * blk * dil + r, blk, stride=dil)
                pending.append((q_ref[hi - blk:hi, :], k_ref[lo:hi, :], v_ref[lo:hi, :],
                                mask1 if j == 0 else mask2, (g, rows)))
                if len(pending) == ATTN_BATCH:
                    run_batch(pending)
                    pending = []
    assert not pending

    l0, l1, l2 = lse_ref[0], lse_ref[1], lse_ref[2]
    mx = jnp.maximum(jnp.maximum(l0, l1), l2)
    w0, w1, w2 = jnp.exp2(l0 - mx), jnp.exp2(l1 - mx), jnp.exp2(l2 - mx)
    o = (w0 * og_ref[0] + w1 * og_ref[1] + w2 * og_ref[2]) / (w0 + w1 + w2)
    o_ref[...] = o.astype(o_ref.dtype)


def _attention(qkv, batch):
    m = qkv[0].shape[0]
    hg = ATTN_HEADS_PER_GROUP

    def col(part):
        return pl.BlockSpec((SEQ, LANES), lambda b, h, o=part * hg: (b, o + h))

    return pl.pallas_call(
        _attn_kernel,
        grid=(batch, hg),
        in_specs=[col(0), col(1), col(2)] * len(ATTN_GROUPS),
        out_specs=pl.BlockSpec((SEQ, LANES), lambda b, h: (b, h)),
        out_shape=jax.ShapeDtypeStruct((m, ATTN_OUT), BF16),
        scratch_shapes=[pltpu.VMEM((len(ATTN_GROUPS), SEQ, LANES), F32)] * 2,
        compiler_params=_params("parallel", "parallel"),
        name="dilated_attn",
    )(*[a for g in qkv for a in (g, g, g)])


MIX_GATE_COLS = 512


def _mix_kernel(*refs):
    nc = D_MODEL // MIX_GATE_COLS
    oa_ref, ob_ref = refs[0], refs[1]
    ga_refs, gb_refs = refs[2:2 + nc], refs[2 + nc:2 + 2 * nc]
    x_ref, wpa_ref, wpb_ref, wo_ref, n2_ref, h_ref, v_ref = refs[2 + 2 * nc:]
    oa, ob = oa_ref[...], ob_ref[...]
    parts = []
    for c in range(nc):
        cols = slice(c * MIX_GATE_COLS, (c + 1) * MIX_GATE_COLS)
        ya = _dot(oa, wpa_ref[:, cols])
        yb = _dot(ob, wpb_ref[:, cols])
        mix = jax.nn.sigmoid(ga_refs[c][...]) * ya + jax.nn.sigmoid(gb_refs[c][...]) * yb
        parts.append(mix.astype(BF16))
    h = x_ref[...] + _dot(jnp.concatenate(parts, axis=1), wo_ref[...])
    h_ref[...] = h
    v_ref[...] = _rmsnorm_rows(h, n2_ref[...]).astype(v_ref.dtype)


def _mix(oa, ob, proj, x2, wpa, wpb, wo, norm2_g, tm=256):
    m, d = x2.shape
    gc = MIX_GATE_COLS
    nc = d // gc
    resident = dict(pipeline_mode=pl.Buffered(1))

    def gate(off, c):
        return pl.BlockSpec((tm, gc), lambda i, o=off * LANES // gc + c: (i, o))

    return pl.pallas_call(
        _mix_kernel,
        grid=(m // tm,),
        in_specs=[
            pl.BlockSpec((tm, HG_WIDTH), lambda i: (i, 0)),
            pl.BlockSpec((tm, ATTN_OUT), lambda i: (i, 0)),
            *[gate(_OFF_GA, c) for c in range(nc)],
            *[gate(_OFF_GB, c) for c in range(nc)],
            pl.BlockSpec((tm, d), lambda i: (i, 0)),
            pl.BlockSpec(wpa.shape, lambda i: (0, 0), **resident),
            pl.BlockSpec(wpb.shape, lambda i: (0, 0), **resident),
            pl.BlockSpec(wo.shape, lambda i: (0, 0), **resident),
            pl.BlockSpec((1, d), lambda i: (0, 0)),
        ],
        out_specs=[pl.BlockSpec((tm, d), lambda i: (i, 0)), pl.BlockSpec((tm, d), lambda i: (i, 0))],
        out_shape=[jax.ShapeDtypeStruct((m, d), F32), jax.ShapeDtypeStruct((m, d), BF16)],
        compiler_params=_params("parallel"),
        name="mix_outproj",
    )(oa, ob, *([proj] * (2 * nc)), x2, wpa, wpb, wo, norm2_g.reshape(1, d))


def _shift_rows(h, prev, k):
    r = pltpu.roll(h, k, axis=0)
    p = pltpu.roll(prev, k, axis=0)
    row8 = lax.broadcasted_iota(jnp.int32, prev.shape, 0)
    top = jnp.where(row8 < k, p, r[:8])
    return jnp.concatenate([top, r[8:]], axis=0)


def _causal_conv3(h, prev, w, b):
    return w[2:3] * h + w[1:2] * _shift_rows(h, prev, 1) + w[0:1] * _shift_rows(h, prev, 2) + b


def _gelu_tanh_gate(x, y):
    c = math.sqrt(2.0 / math.pi)
    w = x * ((-2.0 * c * LOG2E) + (-2.0 * c * 0.044715 * LOG2E) * (x * x))
    return (x * y) * (1.0 / (1.0 + jnp.exp2(w)))


def _ffn_up_kernel(v_ref, halo_ref, wa_ref, wb_ref, cwa_ref, cwb_ref, cba_ref, cbb_ref, o_ref, *, sub):
    tm = v_ref.shape[0]
    hr = halo_ref.shape[0]
    starts_sequence = (pl.program_id(0) * tm) % SEQ == 0
    halo = jnp.where(starts_sequence, jnp.zeros_like(halo_ref[...]), halo_ref[...])
    wa, wb = wa_ref[...].astype(BF16), wb_ref[...].astype(BF16)
    cwa, cwb, cba, cbb = cwa_ref[...], cwb_ref[...], cba_ref[...], cbb_ref[...]
    pa = pb = None
    for r0 in range(0, tm, sub):
        vs = v_ref[r0:r0 + sub, :]
        if r0 == 0:
            vs = jnp.concatenate([halo, vs], axis=0)
        ha, hb = _dot(vs, wa), _dot(vs, wb)
        if r0 == 0:
            pa, pb = ha[hr - 8:hr], hb[hr - 8:hr]
            ha, hb = ha[hr:], hb[hr:]
        ca = _causal_conv3(ha, pa, cwa, cba)
        cb = _causal_conv3(hb, pb, cwb, cbb)
        o_ref[r0:r0 + sub, :] = _gelu_tanh_gate(ca, cb).astype(o_ref.dtype)
        pa, pb = ha[sub - 8:], hb[sub - 8:]


def _ffn_up(v, w_up, conv_w, conv_b, tm=2048, tn=512, sub=512):
    m, d = v.shape
    nj = D_FF // tn
    hb = tm // BF16_SUBLANES
    return pl.pallas_call(
        functools.partial(_ffn_up_kernel, sub=sub),
        grid=(m // tm, nj),
        in_specs=[
            pl.BlockSpec((tm, d), lambda i, j: (i, 0)),
            pl.BlockSpec((BF16_SUBLANES, d), lambda i, j: (jnp.maximum(i * hb - 1, 0), 0)),
            pl.BlockSpec((d, tn), lambda i, j: (0, j)),
            pl.BlockSpec((d, tn), lambda i, j: (0, j + nj)),
            pl.BlockSpec((3, tn), lambda i, j: (0, j)),
            pl.BlockSpec((3, tn), lambda i, j: (0, j + nj)),
            pl.BlockSpec((1, tn), lambda i, j: (0, j)),
            pl.BlockSpec((1, tn), lambda i, j: (0, j + nj)),
        ],
        out_specs=pl.BlockSpec((tm, tn), lambda i, j: (i, j)),
        out_shape=jax.ShapeDtypeStruct((m, D_FF), BF16),
        compiler_params=_params("parallel", "parallel"),
        name="ffn_up_conv_gate",
    )(v, v, w_up, w_up, conv_w, conv_w, conv_b.reshape(1, -1), conv_b.reshape(1, -1))


def _ffn_down_kernel(a_ref, w_ref, h_ref, g_ref, o_ref):
    o_ref[...] = _rmsnorm_rows(h_ref[...] + _dot(a_ref[...], w_ref[...]), g_ref[...])


def _ffn_down(act, w_down, h1, norm_g, tm=512):
    m, kdim = act.shape
    d = w_down.shape[1]
    return pl.pallas_call(
        _ffn_down_kernel,
        grid=(m // tm,),
        in_specs=[
            pl.BlockSpec((tm, kdim), lambda i: (i, 0)),
            pl.BlockSpec((kdim, d), lambda i: (0, 0), pipeline_mode=pl.Buffered(1)),
            pl.BlockSpec((tm, d), lambda i: (i, 0)),
            pl.BlockSpec((1, d), lambda i: (0, 0)),
        ],
        out_specs=pl.BlockSpec((tm, d), lambda i: (i, 0)),
        out_shape=jax.ShapeDtypeStruct((m, d), F32),
        compiler_params=_params("parallel"),
        name="ffn_down_norm",
    )(act, w_down, h1, norm_g.reshape(1, d))


def kernel(x, norm1_g, w_in, hg_lb_param, hg_norm_g, w_proj_a, w_proj_b, w_out,
           norm2_g, w_up, conv_w, conv_b, w_down, norm_f_g):
    batch, seq, d = x.shape
    depth = w_in.shape[0]
    assert (seq, d, depth) == (SEQ, D_MODEL, 1) and w_in.shape[2] == IN_COLS
    h = x.reshape(batch * seq, d)
    for l in range(depth):
        u = _norm_bf16(h, norm1_g[l])
        proj, (wpa, wpb, wo, wdn) = _in_proj(u, w_in[l], (w_proj_a[l], w_proj_b[l], w_out[l], w_down[l]))
        oa = _hgrn2(proj, hg_lb_param, hg_norm_g[l], batch)
        ob = _attention([_attn_proj(u, w_in[l], g) for g in range(len(ATTN_GROUPS))], batch)
        h, v = _mix(oa, ob, proj, h, wpa, wpb, wo, norm2_g[l])
        act = _ffn_up(v, w_up[l], conv_w[l], conv_b[l])
        out = _ffn_down(act, wdn, h, norm_f_g)
    return out.reshape(batch, seq, d)
```

```python
import functools
import math

import numpy as np
import jax
import jax.numpy as jnp
from jax import lax
from jax.experimental import pallas as pl
from jax.experimental.pallas import tpu as pltpu

D_MODEL = 2048
SEQ = 2048
HG_HEADS = 16
HG_DIM = 128
HG_WIDTH = HG_HEADS * HG_DIM
ATTN_GROUPS = ((128, 1), (512, 4), (2048, 16))
ATTN_HEADS_PER_GROUP = 4
HEAD_DIM = 128
ATTN_WIDTH = len(ATTN_GROUPS) * ATTN_HEADS_PER_GROUP * HEAD_DIM
ATTN_OUT = ATTN_HEADS_PER_GROUP * HEAD_DIM
ROPE_THETA = 10000.0
D_FF = 5632
NORM_EPS = 1e-6
NEG_INF = -1e30
IN_COLS = 4 * HG_WIDTH + 3 * ATTN_WIDTH + 2 * D_MODEL

_OFF_HQ = 0
_OFF_HF = HG_WIDTH // 128
_OFF_HI = 2 * HG_WIDTH // 128
_OFF_HG = 3 * HG_WIDTH // 128
_OFF_AQ = 4 * HG_WIDTH // 128
_OFF_AK = _OFF_AQ + ATTN_WIDTH // 128
_OFF_AV = _OFF_AK + ATTN_WIDTH // 128
_OFF_GA = 4 * HG_WIDTH // 128
_OFF_GB = _OFF_GA + D_MODEL // 128

LANES = 128
BF16_SUBLANES = 16
VMEM_LIMIT = 56 * 1024 * 1024
LOG2E = 1.4426950408889634

F32 = jnp.float32
BF16 = jnp.bfloat16


def _dot(a, b):
    return jnp.dot(a, b, preferred_element_type=F32)


def _dot_nt(a, b):
    return lax.dot_general(a, b, (((1,), (1,)), ((), ())), preferred_element_type=F32)


def _dot_tn(a, b):
    return lax.dot_general(a, b, (((0,), (0,)), ((), ())), preferred_element_type=F32)


def _rmsnorm_rows(x, g):
    ms = jnp.mean(x * x, axis=-1, keepdims=True)
    return x * lax.rsqrt(ms + NORM_EPS) * g


def _sigmoid(x):
    return 1.0 / (1.0 + jnp.exp2(x * (-LOG2E)))


def _params(*sem):
    return pltpu.CompilerParams(dimension_semantics=sem, vmem_limit_bytes=VMEM_LIMIT)


def _norm_kernel(x_ref, g_ref, u_ref):
    u_ref[...] = _rmsnorm_rows(x_ref[...], g_ref[...]).astype(u_ref.dtype)


def _norm_bf16(x2, norm_g, tm=512):
    m, d = x2.shape
    return pl.pallas_call(
        _norm_kernel,
        grid=(m // tm,),
        in_specs=[pl.BlockSpec((tm, d), lambda i: (i, 0)), pl.BlockSpec((1, d), lambda i: (0, 0))],
        out_specs=pl.BlockSpec((tm, d), lambda i: (i, 0)),
        out_shape=jax.ShapeDtypeStruct((m, d), BF16),
        compiler_params=_params("parallel"),
        name="norm1",
    )(x2, norm_g.reshape(1, d))


def _inproj_kernel(u_ref, w_ref, *refs):
    n_side = (len(refs) - 1) // 2
    o_ref = refs[n_side]
    o_ref[...] = _dot(u_ref[...], w_ref[...].astype(BF16))
    for src, dst in zip(refs[:n_side], refs[n_side + 1:]):
        dst[...] = src[...].astype(dst.dtype)


def _in_proj(u, w, side_weights, tm=2048, tn=512):
    m, d = u.shape
    n_hg, skip = 4 * HG_WIDTH // tn, 3 * ATTN_WIDTH // tn
    nj = n_hg + 2 * D_MODEL // tn
    steps = (m // tm) * nj
    side_in, side_out, side_shape = [], [], []
    for sw in side_weights:
        rows = -(-sw.shape[0] // steps)
        rows = -(-rows // BF16_SUBLANES) * BF16_SUBLANES
        assert sw.shape[0] % rows == 0
        last = sw.shape[0] // rows - 1
        spec = pl.BlockSpec((rows, sw.shape[1]), lambda i, j, last=last: (jnp.minimum(i * nj + j, last), 0))
        side_in.append(spec)
        side_out.append(spec)
        side_shape.append(jax.ShapeDtypeStruct(sw.shape, BF16))
    outs = pl.pallas_call(
        _inproj_kernel,
        grid=(m // tm, nj),
        in_specs=[
            pl.BlockSpec((tm, d), lambda i, j: (i, 0)),
            pl.BlockSpec((d, tn), lambda i, j: (0, jnp.where(j < n_hg, j, j + skip))),
            *side_in,
        ],
        out_specs=[pl.BlockSpec((tm, tn), lambda i, j: (i, j)), *side_out],
        out_shape=[jax.ShapeDtypeStruct((m, nj * tn), F32), *side_shape],
        compiler_params=_params("arbitrary", "arbitrary"),
        name="in_proj",
    )(u, w, *side_weights)
    return outs[0], outs[1:]


def _attn_proj_kernel(u_ref, w_ref, cos_ref, sin_ref, o_ref, scr, scr2, *, dil, sub):
    w = w_ref[...].astype(BF16)
    tm = u_ref.shape[0]
    length, n = tm // dil, sub // dil
    for r0 in range(0, tm, sub):
        acc = _dot(u_ref[r0:r0 + sub, :], w)
        cs, sn = cos_ref[r0:r0 + sub, :], sin_ref[r0:r0 + sub, :]
        for hh in range(acc.shape[1] // HEAD_DIM):
            ln = slice(hh * HEAD_DIM, (hh + 1) * HEAD_DIM)
            x = acc[:, ln]
            y = x * cs + pltpu.roll(x, HEAD_DIM // 2, axis=1) * sn
            if dil == 1:
                o_ref[r0:r0 + sub, ln] = y.astype(o_ref.dtype)
                continue
            scr[hh] = y
            for r1 in range(4):
                part = scr[hh, pl.ds(r1, sub // 4, stride=4), :]
                if dil == 4:
                    dst = r1 * length + r0 // dil
                    o_ref[dst:dst + n, ln] = part.astype(o_ref.dtype)
                    continue
                scr2[hh, r1] = part
                for r2 in range(4):
                    dst = _attn_slot(r1 + 4 * r2) * length + r0 // dil
                    o_ref[dst:dst + n, ln] = scr2[hh, r1, pl.ds(r2, n, stride=4), :].astype(o_ref.dtype)


def _attn_slot(r):
    return (r % 4) * 4 + r // 4


def _attn_proj(u, w, group, tm=SEQ, sub=512):
    m, d = u.shape
    dil = ATTN_GROUPS[group][1]
    assert dil in (1, 4, 16)
    tn = ATTN_HEADS_PER_GROUP * HEAD_DIM
    half = HEAD_DIM // 2
    inv_freq = jnp.exp(-math.log(ROPE_THETA) * jnp.arange(half, dtype=F32) * (2.0 / HEAD_DIM))
    ang = jnp.arange(SEQ, dtype=F32)[:, None] * inv_freq[None, :]
    cos, sin = jnp.cos(ang), jnp.sin(ang)
    one, zero = jnp.ones((SEQ, HEAD_DIM), F32), jnp.zeros((SEQ, HEAD_DIM), F32)
    cos_t = jnp.stack([jnp.concatenate([cos, cos], axis=1), one])
    sin_t = jnp.stack([jnp.concatenate([-sin, sin], axis=1), zero])
    first = (_OFF_AQ * LANES) // tn + group
    per = ATTN_WIDTH // tn
    table = pl.BlockSpec((None, SEQ, HEAD_DIM), lambda i, j: (j // 2, 0, 0))
    return pl.pallas_call(
        functools.partial(_attn_proj_kernel, dil=dil, sub=sub),
        grid=(m // tm, 3),
        in_specs=[
            pl.BlockSpec((tm, d), lambda i, j: (i, 0)),
            pl.BlockSpec((d, tn), lambda i, j: (0, first + per * j)),
            table, table,
        ],
        out_specs=pl.BlockSpec((tm, tn), lambda i, j: (i, j)),
        out_shape=jax.ShapeDtypeStruct((m, 3 * tn), BF16),
        scratch_shapes=[pltpu.VMEM((tn // HEAD_DIM, sub, HEAD_DIM), F32),
                        pltpu.VMEM((tn // HEAD_DIM, 4, sub // 4, HEAD_DIM), F32)],
        compiler_params=_params("parallel", "parallel"),
        name="attn_proj",
    )(u, w, cos_t, sin_t)


HG_CHUNK = 128
HG_HEADS_PER_STEP = 4
HG_CHUNKS_PER_STEP = 4


def _hgrn_tables(c):
    t = np.arange(c)[:, None]
    s = np.arange(c)[None, :]
    x = t ^ s
    lvl = np.where(x > 0, np.floor(np.log2(np.maximum(x, 1))), -1).astype(np.int32)
    lvl = np.where(s > t, -2, lvl).astype(np.int32)
    tri = (s <= t).astype(np.float32)
    return jnp.asarray(tri, BF16), jnp.asarray(lvl, jnp.int32)


def _neg_abs(d):
    bits = lax.bitcast_convert_type(d, jnp.uint32) | jnp.uint32(0x80000000)
    return lax.bitcast_convert_type(bits, F32)


def _hgrn_chunk(zq, zf, vv, zg, sts, lb, oml, ng, tri, lvl, row):
    heads = range(len(zq))
    c = zq[0].shape[0]
    f = [lb[i] + oml[i] * _sigmoid(zf[i]) for i in heads]
    kk = [1.0 - f[i] for i in heads]
    qq = [zq[i] * _sigmoid(zq[i]) for i in heads]
    gl = [jnp.log(f[i]) * LOG2E for i in heads]

    gcum = []
    for i in heads:
        g1 = gl[i].astype(BF16)
        g2 = (gl[i] - g1.astype(F32)).astype(BF16)
        gc = _dot(tri, jnp.concatenate([g1, g2], axis=1))
        gcum.append(gc[:, :LANES] + gc[:, LANES:])
    glast = [gcum[i][c - 1:c, :] for i in heads]

    qb = [qq[i].astype(BF16) for i in heads]
    kb = [kk[i].astype(BF16) for i in heads]
    st_in, a = list(sts), []
    for i in heads:
        kd = (kk[i] * jnp.exp2(glast[i] - gcum[i])).astype(BF16)
        st_in.append(st_in[i] * jnp.exp2(glast[i]) + _dot_tn(vv[i], kd))
        a.append(jnp.where(lvl == -1, _dot_nt(qb[i], kb[i]), 0.0))

    h = c // 2
    while h >= 1:
        in_level = lvl == int(math.log2(h))
        for i in heads:
            if h >= 4:
                g3d = gcum[i].reshape(c // (2 * h), 2 * h, LANES)
                ref = jnp.broadcast_to(g3d[:, h - 1:h, :], g3d.shape).reshape(c, LANES)
                x = _neg_abs(gcum[i] - ref)
            elif h == 2:
                pos = row & 3
                up = pltpu.roll(gl[i], c - 1, axis=0)
                dn = pltpu.roll(gl[i], 1, axis=0)
                x = jnp.where(pos == 0, up, jnp.where(pos == 1, 0.0, jnp.where(pos == 2, gl[i], gl[i] + dn)))
            else:
                x = jnp.where((row & 1) == 1, gl[i], 0.0)
            e = jnp.exp2(x).astype(BF16)
            a[i] = jnp.where(in_level, _dot_nt(qb[i] * e, kb[i] * e), a[i])
        h //= 2
    o = []
    for i in heads:
        qg = (qq[i] * jnp.exp2(gcum[i])).astype(BF16)
        o.append(_dot_nt(qg, st_in[i].astype(BF16)) + _dot(a[i].astype(BF16), vv[i]))
    y = [_rmsnorm_rows(o[i], ng) * (zg[i] * _sigmoid(zg[i])) for i in heads]
    return y, st_in[len(zq):]


def _hgrn_kernel(q_ref, f_ref, i_ref, g_ref, lbp_ref, ng_ref, tri_ref, lvl_ref, o_ref, *, chunk, heads):
    c = chunk
    seq = q_ref.shape[0]
    p = lbp_ref[...]
    pe = jnp.exp(p - jnp.max(p, axis=0, keepdims=True))
    lb = pe[0:1, :] / jnp.sum(pe, axis=0, keepdims=True)
    oml = 1.0 - lb
    ng = ng_ref[...]
    row = lax.broadcasted_iota(jnp.int32, (c, LANES), 0)
    lanes = [slice(hh * LANES, (hh + 1) * LANES) for hh in range(heads)]
    nch = HG_CHUNKS_PER_STEP

    def body(ci, sts):
        ent = [(pl.ds(pl.multiple_of((ci * nch + ch) * c, c), c), ln) for ch in range(nch) for ln in lanes]
        y, new = _hgrn_chunk(
            [q_ref[rows, ln] for rows, ln in ent], [f_ref[rows, ln] for rows, ln in ent],
            [i_ref[rows, ln].astype(BF16) for rows, ln in ent], [g_ref[rows, ln] for rows, ln in ent],
            sts, [lb[:, ln] for _, ln in ent], [oml[:, ln] for _, ln in ent],
            ng, tri_ref[...], lvl_ref[...], row)
        for (rows, ln), yy in zip(ent, y):
            o_ref[rows, ln] = yy.astype(o_ref.dtype)
        return tuple(new)

    lax.fori_loop(0, seq // (c * nch), body, tuple(jnp.zeros((HG_DIM, HG_DIM), F32) for _ in range(heads)))


def _hgrn2(proj, lb_param, norm_g, batch):
    c, hp = HG_CHUNK, HG_HEADS_PER_STEP
    tri, lvl = _hgrn_tables(c)
    m = proj.shape[0]
    w = hp * LANES

    def col(off):
        return pl.BlockSpec((SEQ, w), lambda b, h, o=off // hp: (b, o + h))

    return pl.pallas_call(
        functools.partial(_hgrn_kernel, chunk=c, heads=hp),
        grid=(batch, HG_HEADS // hp),
        in_specs=[
            col(_OFF_HQ), col(_OFF_HF), col(_OFF_HI), col(_OFF_HG),
            pl.BlockSpec((lb_param.shape[0], w), lambda b, h: (0, h)),
            pl.BlockSpec((1, LANES), lambda b, h: (0, 0)),
            pl.BlockSpec((c, c), lambda b, h: (0, 0)),
            pl.BlockSpec((c, c), lambda b, h: (0, 0)),
        ],
        out_specs=pl.BlockSpec((SEQ, w), lambda b, h: (b, h)),
        out_shape=jax.ShapeDtypeStruct((m, HG_WIDTH), BF16),
        compiler_params=_params("parallel", "parallel"),
        name="hgrn2",
    )(proj, proj, proj, proj, lb_param, norm_g.reshape(1, LANES), tri, lvl)


ATTN_BLOCK = 128
ATTN_BATCH = 8


def _attn_kernel(q0, k0, v0, q1, k1, v1, q2, k2, v2, o_ref, og_ref, lse_ref):
    blk = ATTN_BLOCK
    scale = HEAD_DIM ** -0.5
    qi = lax.broadcasted_iota(jnp.int32, (blk, 2 * blk), 0)
    ki = lax.broadcasted_iota(jnp.int32, (blk, 2 * blk), 1)
    mask2 = (ki >= qi) & (ki <= qi + blk)
    mask1 = (lax.broadcasted_iota(jnp.int32, (blk, blk), 1)
             <= lax.broadcasted_iota(jnp.int32, (blk, blk), 0))

    def run_batch(items):
        s = [_dot_nt(qj, kw) * (scale * LOG2E) for qj, kw, _, _, _ in items]
        ps, mxs = [], []
        for (_, _, _, mask, _), sj in zip(items, s):
            sj = jnp.where(mask, sj, NEG_INF)
            mx = jnp.max(sj, axis=-1, keepdims=True)
            ps.append(jnp.exp2(sj - mx).astype(BF16))
            mxs.append(mx)
        os = [_dot(p, jnp.concatenate([vw, jnp.ones_like(vw)], axis=1)) for p, (_, _, vw, _, _) in zip(ps, items)]
        for (_, _, _, _, (g, rows)), o, mx in zip(items, os, mxs):
            den = o[:, HEAD_DIM:]
            og_ref.at[g][rows, :] = o[:, :HEAD_DIM] * (1.0 / den)
            lse_ref.at[g][rows, :] = mx + jnp.log(den) * LOG2E

    refs = ((q0, k0, v0), (q1, k1, v1), (q2, k2, v2))
    pending = []
    for g, (window, dil) in enumerate(ATTN_GROUPS):
        assert window // dil == blk
        q_ref, k_ref, v_ref = refs[g]
        length = SEQ // dil
        for r in range(dil):
            base = (_attn_slot(r) if dil == 16 else r) * length
            for j in range(length // blk):
                lo, hi = base + max(j - 1, 0) * blk, base + (j + 1) * blk
                rows = pl.ds(j * blk, blk) if dil == 1 else pl.ds(j * blk * dil + r, blk, stride=dil)
                pending.append((q_ref[hi - blk:hi, :], k_ref[lo:hi, :], v_ref[lo:hi, :],
                                mask1 if j == 0 else mask2, (g, rows)))
                if len(pending) == ATTN_BATCH:
                    run_batch(pending)
                    pending = []
    assert not pending

    l0, l1, l2 = lse_ref[0], lse_ref[1], lse_ref[2]
    mx = jnp.maximum(jnp.maximum(l0, l1), l2)
    w0, w1, w2 = jnp.exp2(l0 - mx), jnp.exp2(l1 - mx), jnp.exp2(l2 - mx)
    o = (w0 * og_ref[0] + w1 * og_ref[1] + w2 * og_ref[2]) / (w0 + w1 + w2)
    o_ref[...] = o.astype(o_ref.dtype)


def _attention(qkv, batch):
    m = qkv[0].shape[0]
    hg = ATTN_HEADS_PER_GROUP

    def col(part):
        return pl.BlockSpec((SEQ, LANES), lambda b, h, o=part * hg: (b, o + h))

    return pl.pallas_call(
        _attn_kernel,
        grid=(batch, hg),
        in_specs=[col(0), col(1), col(2)] * len(ATTN_GROUPS),
        out_specs=pl.BlockSpec((SEQ, LANES), lambda b, h: (b, h)),
        out_shape=jax.ShapeDtypeStruct((m, ATTN_OUT), BF16),
        scratch_shapes=[pltpu.VMEM((len(ATTN_GROUPS), SEQ, LANES), F32)] * 2,
        compiler_params=_params("parallel", "parallel"),
        name="dilated_attn",
    )(*[a for g in qkv for a in (g, g, g)])


MIX_GATE_COLS = 512


def _mix_kernel(*refs):
    nc = D_MODEL // MIX_GATE_COLS
    oa_ref, ob_ref = refs[0], refs[1]
    ga_refs, gb_refs = refs[2:2 + nc], refs[2 + nc:2 + 2 * nc]
    x_ref, wpa_ref, wpb_ref, wo_ref, n2_ref, h_ref, v_ref = refs[2 + 2 * nc:]
    oa, ob = oa_ref[...], ob_ref[...]
    parts = []
    for c in range(nc):
        cols = slice(c * MIX_GATE_COLS, (c + 1) * MIX_GATE_COLS)
        ya = _dot(oa, wpa_ref[:, cols])
        yb = _dot(ob, wpb_ref[:, cols])
        mix = jax.nn.sigmoid(ga_refs[c][...]) * ya + jax.nn.sigmoid(gb_refs[c][...]) * yb
        parts.append(mix.astype(BF16))
    h = x_ref[...] + _dot(jnp.concatenate(parts, axis=1), wo_ref[...])
    h_ref[...] = h
    v_ref[...] = _rmsnorm_rows(h, n2_ref[...]).astype(v_ref.dtype)


def _mix(oa, ob, proj, x2, wpa, wpb, wo, norm2_g, tm=256):
    m, d = x2.shape
    gc = MIX_GATE_COLS
    nc = d // gc
    resident = dict(pipeline_mode=pl.Buffered(1))

    def gate(off, c):
        return pl.BlockSpec((tm, gc), lambda i, o=off * LANES // gc + c: (i, o))

    return pl.pallas_call(
        _mix_kernel,
        grid=(m // tm,),
        in_specs=[
            pl.BlockSpec((tm, HG_WIDTH), lambda i: (i, 0)),
            pl.BlockSpec((tm, ATTN_OUT), lambda i: (i, 0)),
            *[gate(_OFF_GA, c) for c in range(nc)],
            *[gate(_OFF_GB, c) for c in range(nc)],
            pl.BlockSpec((tm, d), lambda i: (i, 0)),
            pl.BlockSpec(wpa.shape, lambda i: (0, 0), **resident),
            pl.BlockSpec(wpb.shape, lambda i: (0, 0), **resident),
            pl.BlockSpec(wo.shape, lambda i: (0, 0), **resident),
            pl.BlockSpec((1, d), lambda i: (0, 0)),
        ],
        out_specs=[pl.BlockSpec((tm, d), lambda i: (i, 0)), pl.BlockSpec((tm, d), lambda i: (i, 0))],
        out_shape=[jax.ShapeDtypeStruct((m, d), F32), jax.ShapeDtypeStruct((m, d), BF16)],
        compiler_params=_params("parallel"),
        name="mix_outproj",
    )(oa, ob, *([proj] * (2 * nc)), x2, wpa, wpb, wo, norm2_g.reshape(1, d))


def _shift_rows(h, prev, k):
    r = pltpu.roll(h, k, axis=0)
    p = pltpu.roll(prev, k, axis=0)
    row8 = lax.broadcasted_iota(jnp.int32, prev.shape, 0)
    top = jnp.where(row8 < k, p, r[:8])
    return jnp.concatenate([top, r[8:]], axis=0)


def _causal_conv3(h, prev, w, b):
    return w[2:3] * h + w[1:2] * _shift_rows(h, prev, 1) + w[0:1] * _shift_rows(h, prev, 2) + b


def _gelu_tanh_gate(x, y):
    c = math.sqrt(2.0 / math.pi)
    w = x * ((-2.0 * c * LOG2E) + (-2.0 * c * 0.044715 * LOG2E) * (x * x))
    return (x * y) * (1.0 / (1.0 + jnp.exp2(w)))


def _ffn_up_kernel(v_ref, halo_ref, wa_ref, wb_ref, cwa_ref, cwb_ref, cba_ref, cbb_ref, o_ref, *, sub):
    tm = v_ref.shape[0]
    hr = halo_ref.shape[0]
    starts_sequence = (pl.program_id(0) * tm) % SEQ == 0
    halo = jnp.where(starts_sequence, jnp.zeros_like(halo_ref[...]), halo_ref[...])
    wa, wb = wa_ref[...].astype(BF16), wb_ref[...].astype(BF16)
    cwa, cwb, cba, cbb = cwa_ref[...], cwb_ref[...], cba_ref[...], cbb_ref[...]
    pa = pb = None
    for r0 in range(0, tm, sub):
        vs = v_ref[r0:r0 + sub, :]
        if r0 == 0:
            vs = jnp.concatenate([halo, vs], axis=0)
        ha, hb = _dot(vs, wa), _dot(vs, wb)
        if r0 == 0:
            pa, pb = ha[hr - 8:hr], hb[hr - 8:hr]
            ha, hb = ha[hr:], hb[hr:]
        ca = _causal_conv3(ha, pa, cwa, cba)
        cb = _causal_conv3(hb, pb, cwb, cbb)
        o_ref[r0:r0 + sub, :] = _gelu_tanh_gate(ca, cb).astype(o_ref.dtype)
        pa, pb = ha[sub - 8:], hb[sub - 8:]


def _ffn_up(v, w_up, conv_w, conv_b, tm=2048, tn=512, sub=512):
    m, d = v.shape
    nj = D_FF // tn
    hb = tm // BF16_SUBLANES
    return pl.pallas_call(
        functools.partial(_ffn_up_kernel, sub=sub),
        grid=(m // tm, nj),
        in_specs=[
            pl.BlockSpec((tm, d), lambda i, j: (i, 0)),
            pl.BlockSpec((BF16_SUBLANES, d), lambda i, j: (jnp.maximum(i * hb - 1, 0), 0)),
            pl.BlockSpec((d, tn), lambda i, j: (0, j)),
            pl.BlockSpec((d, tn), lambda i, j: (0, j + nj)),
            pl.BlockSpec((3, tn), lambda i, j: (0, j)),
            pl.BlockSpec((3, tn), lambda i, j: (0, j + nj)),
            pl.BlockSpec((1, tn), lambda i, j: (0, j)),
            pl.BlockSpec((1, tn), lambda i, j: (0, j + nj)),
        ],
        out_specs=pl.BlockSpec((tm, tn), lambda i, j: (i, j)),
        out_shape=jax.ShapeDtypeStruct((m, D_FF), BF16),
        compiler_params=_params("parallel", "parallel"),
        name="ffn_up_conv_gate",
    )(v, v, w_up, w_up, conv_w, conv_w, conv_b.reshape(1, -1), conv_b.reshape(1, -1))


def _ffn_down_kernel(a_ref, w_ref, h_ref, g_ref, o_ref):
    o_ref[...] = _rmsnorm_rows(h_ref[...] + _dot(a_ref[...], w_ref[...]), g_ref[...])


def _ffn_down(act, w_down, h1, norm_g, tm=512):
    m, kdim = act.shape
    d = w_down.shape[1]
    return pl.pallas_call(
        _ffn_down_kernel,
        grid=(m // tm,),
        in_specs=[
            pl.BlockSpec((tm, kdim), lambda i: (i, 0)),
            pl.BlockSpec((kdim, d), lambda i: (0, 0), pipeline_mode=pl.Buffered(1)),
            pl.BlockSpec((tm, d), lambda i: (i, 0)),
            pl.BlockSpec((1, d), lambda i: (0, 0)),
        ],
        out_specs=pl.BlockSpec((tm, d), lambda i: (i, 0)),
        out_shape=jax.ShapeDtypeStruct((m, d), F32),
        compiler_params=_params("parallel"),
        name="ffn_down_norm",
    )(act, w_down, h1, norm_g.reshape(1, d))


def kernel(x, norm1_g, w_in, hg_lb_param, hg_norm_g, w_proj_a, w_proj_b, w_out,
           norm2_g, w_up, conv_w, conv_b, w_down, norm_f_g):
    batch, seq, d = x.shape
    depth = w_in.shape[0]
    assert (seq, d, depth) == (SEQ, D_MODEL, 1) and w_in.shape[2] == IN_COLS
    h = x.reshape(batch * seq, d)
    for l in range(depth):
        u = _norm_bf16(h, norm1_g[l])
        proj, (wpa, wpb, wo, wup, wdn) = _in_proj(
            u, w_in[l], (w_proj_a[l], w_proj_b[l], w_out[l], w_up[l], w_down[l]))
        oa = _hgrn2(proj, hg_lb_param, hg_norm_g[l], batch)
        ob = _attention([_attn_proj(u, w_in[l], g) for g in range(len(ATTN_GROUPS))], batch)
        h, v = _mix(oa, ob, proj, h, wpa, wpb, wo, norm2_g[l])
        act = _ffn_up(v, wup, conv_w[l], conv_b[l])
        out = _ffn_down(act, wdn, h, norm_f_g)
    return out.reshape(batch, seq, d)
```

```python
import functools
import math

import numpy as np
import jax
import jax.numpy as jnp
from jax import lax
from jax.experimental import pallas as pl
from jax.experimental.pallas import tpu as pltpu

D_MODEL = 2048
SEQ = 2048
HG_HEADS = 16
HG_DIM = 128
HG_WIDTH = HG_HEADS * HG_DIM
ATTN_GROUPS = ((128, 1), (512, 4), (2048, 16))
ATTN_HEADS_PER_GROUP = 4
HEAD_DIM = 128
ATTN_WIDTH = len(ATTN_GROUPS) * ATTN_HEADS_PER_GROUP * HEAD_DIM
ATTN_OUT = ATTN_HEADS_PER_GROUP * HEAD_DIM
ROPE_THETA = 10000.0
D_FF = 5632
NORM_EPS = 1e-6
NEG_INF = -1e30
IN_COLS = 4 * HG_WIDTH + 3 * ATTN_WIDTH + 2 * D_MODEL

_OFF_HQ = 0
_OFF_HF = HG_WIDTH // 128
_OFF_HI = 2 * HG_WIDTH // 128
_OFF_HG = 3 * HG_WIDTH // 128
_OFF_AQ = 4 * HG_WIDTH // 128
_OFF_AK = _OFF_AQ + ATTN_WIDTH // 128
_OFF_AV = _OFF_AK + ATTN_WIDTH // 128
_OFF_GA = 4 * HG_WIDTH // 128
_OFF_GB = _OFF_GA + D_MODEL // 128

LANES = 128
BF16_SUBLANES = 16
VMEM_LIMIT = 56 * 1024 * 1024
LOG2E = 1.4426950408889634

F32 = jnp.float32
BF16 = jnp.bfloat16


def _dot(a, b):
    return jnp.dot(a, b, preferred_element_type=F32)


def _dot_nt(a, b):
    return lax.dot_general(a, b, (((1,), (1,)), ((), ())), preferred_element_type=F32)


def _dot_tn(a, b):
    return lax.dot_general(a, b, (((0,), (0,)), ((), ())), preferred_element_type=F32)


def _rmsnorm_rows(x, g):
    ms = jnp.mean(x * x, axis=-1, keepdims=True)
    return x * lax.rsqrt(ms + NORM_EPS) * g


def _sigmoid(x):
    return 1.0 / (1.0 + jnp.exp2(x * (-LOG2E)))


def _params(*sem):
    return pltpu.CompilerParams(dimension_semantics=sem, vmem_limit_bytes=VMEM_LIMIT)


def _norm_kernel(x_ref, g_ref, u_ref):
    u_ref[...] = _rmsnorm_rows(x_ref[...], g_ref[...]).astype(u_ref.dtype)


def _norm_bf16(x2, norm_g, tm=1024):
    m, d = x2.shape
    return pl.pallas_call(
        _norm_kernel,
        grid=(m // tm,),
        in_specs=[pl.BlockSpec((tm, d), lambda i: (i, 0)), pl.BlockSpec((1, d), lambda i: (0, 0))],
        out_specs=pl.BlockSpec((tm, d), lambda i: (i, 0)),
        out_shape=jax.ShapeDtypeStruct((m, d), BF16),
        compiler_params=_params("parallel"),
        name="norm1",
    )(x2, norm_g.reshape(1, d))


def _inproj_kernel(u_ref, w_ref, *refs):
    n_side = (len(refs) - 1) // 2
    o_ref = refs[n_side]
    o_ref[...] = _dot(u_ref[...], w_ref[...].astype(BF16))
    for src, dst in zip(refs[:n_side], refs[n_side + 1:]):
        dst[...] = src[...].astype(dst.dtype)


def _in_proj(u, w, side_weights, tm=4096, tn=512):
    m, d = u.shape
    n_hg, skip = 4 * HG_WIDTH // tn, 3 * ATTN_WIDTH // tn
    nj = n_hg + 2 * D_MODEL // tn
    steps = (m // tm) * nj
    side_in, side_out, side_shape = [], [], []
    for sw in side_weights:
        rows = -(-sw.shape[0] // steps)
        rows = -(-rows // BF16_SUBLANES) * BF16_SUBLANES
        while sw.shape[0] % rows:
            rows += BF16_SUBLANES
        last = sw.shape[0] // rows - 1
        spec = pl.BlockSpec((rows, sw.shape[1]), lambda i, j, last=last: (jnp.minimum(i * nj + j, last), 0))
        side_in.append(spec)
        side_out.append(spec)
        side_shape.append(jax.ShapeDtypeStruct(sw.shape, BF16))
    outs = pl.pallas_call(
        _inproj_kernel,
        grid=(m // tm, nj),
        in_specs=[
            pl.BlockSpec((tm, d), lambda i, j: (i, 0), pipeline_mode=pl.Buffered(1)),
            pl.BlockSpec((d, tn), lambda i, j: (0, jnp.where(j < n_hg, j, j + skip))),
            *side_in,
        ],
        out_specs=[pl.BlockSpec((tm, tn), lambda i, j: (i, j)), *side_out],
        out_shape=[jax.ShapeDtypeStruct((m, nj * tn), F32), *side_shape],
        compiler_params=_params("arbitrary", "arbitrary"),
        name="in_proj",
    )(u, w, *side_weights)
    return outs[0], outs[1:]


def _attn_proj_kernel(u_ref, w_ref, cos_ref, sin_ref, o_ref, scr, scr2, *, dil, sub):
    w = w_ref[...].astype(BF16)
    tm = u_ref.shape[0]
    length, n = tm // dil, sub // dil
    for r0 in range(0, tm, sub):
        acc = _dot(u_ref[r0:r0 + sub, :], w)
        cs, sn = cos_ref[r0:r0 + sub, :], sin_ref[r0:r0 + sub, :]
        for hh in range(acc.shape[1] // HEAD_DIM):
            ln = slice(hh * HEAD_DIM, (hh + 1) * HEAD_DIM)
            x = acc[:, ln]
            y = x * cs + pltpu.roll(x, HEAD_DIM // 2, axis=1) * sn
            if dil == 1:
                o_ref[r0:r0 + sub, ln] = y.astype(o_ref.dtype)
                continue
            scr[hh] = y
            for r1 in range(4):
                part = scr[hh, pl.ds(r1, sub // 4, stride=4), :]
                if dil == 4:
                    dst = r1 * length + r0 // dil
                    o_ref[dst:dst + n, ln] = part.astype(o_ref.dtype)
                    continue
                scr2[hh, r1] = part
                for r2 in range(4):
                    dst = _attn_slot(r1 + 4 * r2) * length + r0 // dil
                    o_ref[dst:dst + n, ln] = scr2[hh, r1, pl.ds(r2, n, stride=4), :].astype(o_ref.dtype)


def _attn_slot(r):
    return (r % 4) * 4 + r // 4


def _attn_proj(u, w, group, tm=SEQ, sub=512):
    m, d = u.shape
    dil = ATTN_GROUPS[group][1]
    assert dil in (1, 4, 16)
    tn = ATTN_HEADS_PER_GROUP * HEAD_DIM
    half = HEAD_DIM // 2
    inv_freq = jnp.exp(-math.log(ROPE_THETA) * jnp.arange(half, dtype=F32) * (2.0 / HEAD_DIM))
    ang = jnp.arange(SEQ, dtype=F32)[:, None] * inv_freq[None, :]
    cos, sin = jnp.cos(ang), jnp.sin(ang)
    one, zero = jnp.ones((SEQ, HEAD_DIM), F32), jnp.zeros((SEQ, HEAD_DIM), F32)
    cos_t = jnp.stack([jnp.concatenate([cos, cos], axis=1), one])
    sin_t = jnp.stack([jnp.concatenate([-sin, sin], axis=1), zero])
    first = (_OFF_AQ * LANES) // tn + group
    per = ATTN_WIDTH // tn
    table = pl.BlockSpec((None, SEQ, HEAD_DIM), lambda i, j: (j // 2, 0, 0))
    return pl.pallas_call(
        functools.partial(_attn_proj_kernel, dil=dil, sub=sub),
        grid=(m // tm, 3),
        in_specs=[
            pl.BlockSpec((tm, d), lambda i, j: (i, 0)),
            pl.BlockSpec((d, tn), lambda i, j: (0, first + per * j)),
            table, table,
        ],
        out_specs=pl.BlockSpec((tm, tn), lambda i, j: (i, j)),
        out_shape=jax.ShapeDtypeStruct((m, 3 * tn), BF16),
        scratch_shapes=[pltpu.VMEM((tn // HEAD_DIM, sub, HEAD_DIM), F32),
                        pltpu.VMEM((tn // HEAD_DIM, 4, sub // 4, HEAD_DIM), F32)],
        compiler_params=_params("parallel", "parallel"),
        name="attn_proj",
    )(u, w, cos_t, sin_t)


HG_CHUNK = 128
HG_HEADS_PER_STEP = 4
HG_CHUNKS_PER_STEP = 4


def _hgrn_tables(c):
    t = np.arange(c)[:, None]
    s = np.arange(c)[None, :]
    x = t ^ s
    lvl = np.where(x > 0, np.floor(np.log2(np.maximum(x, 1))), -1).astype(np.int32)
    lvl = np.where(s > t, -2, lvl).astype(np.int32)
    tri = (s <= t).astype(np.float32)
    return jnp.asarray(tri, BF16), jnp.asarray(lvl, jnp.int32)


def _neg_abs(d):
    bits = lax.bitcast_convert_type(d, jnp.uint32) | jnp.uint32(0x80000000)
    return lax.bitcast_convert_type(bits, F32)


def _hgrn_chunk(zq, zf, vv, zg, sts, lb, oml, ng, tri, lvl, row):
    heads = range(len(zq))
    c = zq[0].shape[0]
    f = [lb[i] + oml[i] * _sigmoid(zf[i]) for i in heads]
    kk = [1.0 - f[i] for i in heads]
    qq = [zq[i] * _sigmoid(zq[i]) for i in heads]
    gl = [jnp.log(f[i]) * LOG2E for i in heads]

    gcum = []
    for i in heads:
        g1 = gl[i].astype(BF16)
        g2 = (gl[i] - g1.astype(F32)).astype(BF16)
        gc = _dot(tri, jnp.concatenate([g1, g2], axis=1))
        gcum.append(gc[:, :LANES] + gc[:, LANES:])
    glast = [gcum[i][c - 1:c, :] for i in heads]

    qb = [qq[i].astype(BF16) for i in heads]
    kb = [kk[i].astype(BF16) for i in heads]
    st_in, a = list(sts), []
    for i in heads:
        kd = (kk[i] * jnp.exp2(glast[i] - gcum[i])).astype(BF16)
        st_in.append(st_in[i] * jnp.exp2(glast[i]) + _dot_tn(vv[i], kd))
        a.append(jnp.where(lvl == -1, _dot_nt(qb[i], kb[i]), 0.0))

    h = c // 2
    while h >= 1:
        in_level = lvl == int(math.log2(h))
        for i in heads:
            if h >= 4:
                g3d = gcum[i].reshape(c // (2 * h), 2 * h, LANES)
                ref = jnp.broadcast_to(g3d[:, h - 1:h, :], g3d.shape).reshape(c, LANES)
                x = _neg_abs(gcum[i] - ref)
            elif h == 2:
                pos = row & 3
                up = pltpu.roll(gl[i], c - 1, axis=0)
                dn = pltpu.roll(gl[i], 1, axis=0)
                x = jnp.where(pos == 0, up, jnp.where(pos == 1, 0.0, jnp.where(pos == 2, gl[i], gl[i] + dn)))
            else:
                x = jnp.where((row & 1) == 1, gl[i], 0.0)
            e = jnp.exp2(x).astype(BF16)
            a[i] = jnp.where(in_level, _dot_nt(qb[i] * e, kb[i] * e), a[i])
        h //= 2
    o = []
    for i in heads:
        qg = (qq[i] * jnp.exp2(gcum[i])).astype(BF16)
        o.append(_dot_nt(qg, st_in[i].astype(BF16)) + _dot(a[i].astype(BF16), vv[i]))
    y = [_rmsnorm_rows(o[i], ng) * (zg[i] * _sigmoid(zg[i])) for i in heads]
    return y, st_in[len(zq):]


def _hgrn_kernel(q_ref, f_ref, i_ref, g_ref, lbp_ref, ng_ref, tri_ref, lvl_ref, o_ref, *, chunk, heads):
    c = chunk
    seq = q_ref.shape[0]
    p = lbp_ref[...]
    pe = jnp.exp(p - jnp.max(p, axis=0, keepdims=True))
    lb = pe[0:1, :] / jnp.sum(pe, axis=0, keepdims=True)
    oml = 1.0 - lb
    ng = ng_ref[...]
    row = lax.broadcasted_iota(jnp.int32, (c, LANES), 0)
    lanes = [slice(hh * LANES, (hh + 1) * LANES) for hh in range(heads)]
    nch = HG_CHUNKS_PER_STEP

    def body(ci, sts):
        ent = [(pl.ds(pl.multiple_of((ci * nch + ch) * c, c), c), ln) for ch in range(nch) for ln in lanes]
        y, new = _hgrn_chunk(
            [q_ref[rows, ln] for rows, ln in ent], [f_ref[rows, ln] for rows, ln in ent],
            [i_ref[rows, ln].astype(BF16) for rows, ln in ent], [g_ref[rows, ln] for rows, ln in ent],
            sts, [lb[:, ln] for _, ln in ent], [oml[:, ln] for _, ln in ent],
            ng, tri_ref[...], lvl_ref[...], row)
        for (rows, ln), yy in zip(ent, y):
            o_ref[rows, ln] = yy.astype(o_ref.dtype)
        return tuple(new)

    lax.fori_loop(0, seq // (c * nch), body, tuple(jnp.zeros((HG_DIM, HG_DIM), F32) for _ in range(heads)))


def _hgrn2(proj, lb_param, norm_g, batch):
    c, hp = HG_CHUNK, HG_HEADS_PER_STEP
    tri, lvl = _hgrn_tables(c)
    m = proj.shape[0]
    w = hp * LANES

    def col(off):
        return pl.BlockSpec((SEQ, w), lambda b, h, o=off // hp: (b, o + h))

    return pl.pallas_call(
        functools.partial(_hgrn_kernel, chunk=c, heads=hp),
        grid=(batch, HG_HEADS // hp),
        in_specs=[
            col(_OFF_HQ), col(_OFF_HF), col(_OFF_HI), col(_OFF_HG),
            pl.BlockSpec((lb_param.shape[0], w), lambda b, h: (0, h)),
            pl.BlockSpec((1, LANES), lambda b, h: (0, 0)),
            pl.BlockSpec((c, c), lambda b, h: (0, 0)),
            pl.BlockSpec((c, c), lambda b, h: (0, 0)),
        ],
        out_specs=pl.BlockSpec((SEQ, w), lambda b, h: (b, h)),
        out_shape=jax.ShapeDtypeStruct((m, HG_WIDTH), BF16),
        compiler_params=_params("parallel", "parallel"),
        name="hgrn2",
    )(proj, proj, proj, proj, lb_param, norm_g.reshape(1, LANES), tri, lvl)


ATTN_BLOCK = 128
ATTN_BATCH = 8


def _attn_kernel(q0, k0, v0, q1, k1, v1, q2, k2, v2, o_ref, og_ref, lse_ref):
    blk = ATTN_BLOCK
    scale = HEAD_DIM ** -0.5
    qi = lax.broadcasted_iota(jnp.int32, (blk, 2 * blk), 0)
    ki = lax.broadcasted_iota(jnp.int32, (blk, 2 * blk), 1)
    mask2 = (ki >= qi) & (ki <= qi + blk)
    mask1 = (lax.broadcasted_iota(jnp.int32, (blk, blk), 1)
             <= lax.broadcasted_iota(jnp.int32, (blk, blk), 0))

    def run_batch(items):
        s = [_dot_nt(qj, kw) * (scale * LOG2E) for qj, kw, _, _, _ in items]
        ps, mxs = [], []
        for (_, _, _, mask, _), sj in zip(items, s):
            sj = jnp.where(mask, sj, NEG_INF)
            mx = jnp.max(sj, axis=-1, keepdims=True)
            ps.append(jnp.exp2(sj - mx).astype(BF16))
            mxs.append(mx)
        os = [_dot(p, jnp.concatenate([vw, jnp.ones_like(vw)], axis=1)) for p, (_, _, vw, _, _) in zip(ps, items)]
        for (_, _, _, _, (g, rows)), o, mx in zip(items, os, mxs):
            den = o[:, HEAD_DIM:]
            og_ref.at[g][rows, :] = o[:, :HEAD_DIM] * (1.0 / den)
            lse_ref.at[g][rows, :] = mx + jnp.log(den) * LOG2E

    refs = ((q0, k0, v0), (q1, k1, v1), (q2, k2, v2))
    pending = []
    for g, (window, dil) in enumerate(ATTN_GROUPS):
        assert window // dil == blk
        q_ref, k_ref, v_ref = refs[g]
        length = SEQ // dil
        for r in range(dil):
            base = (_attn_slot(r) if dil == 16 else r) * length
            for j in range(length // blk):
                lo, hi = base + max(j - 1, 0) * blk, base + (j + 1) * blk
                rows = pl.ds(j * blk, blk) if dil == 1 else pl.ds(j * blk * dil + r, blk, stride=dil)
                pending.append((q_ref[hi - blk:hi, :], k_ref[lo:hi, :], v_ref[lo:hi, :],
                                mask1 if j == 0 else mask2, (g, rows)))
                if len(pending) == ATTN_BATCH:
                    run_batch(pending)
                    pending = []
    assert not pending

    l0, l1, l2 = lse_ref[0], lse_ref[1], lse_ref[2]
    mx = jnp.maximum(jnp.maximum(l0, l1), l2)
    w0, w1, w2 = jnp.exp2(l0 - mx), jnp.exp2(l1 - mx), jnp.exp2(l2 - mx)
    o = (w0 * og_ref[0] + w1 * og_ref[1] + w2 * og_ref[2]) / (w0 + w1 + w2)
    o_ref[...] = o.astype(o_ref.dtype)


def _attention(qkv, batch):
    m = qkv[0].shape[0]
    hg = ATTN_HEADS_PER_GROUP

    def col(part):
        return pl.BlockSpec((SEQ, LANES), lambda b, h, o=part * hg: (b, o + h))

    return pl.pallas_call(
        _attn_kernel,
        grid=(batch, hg),
        in_specs=[col(0), col(1), col(2)] * len(ATTN_GROUPS),
        out_specs=pl.BlockSpec((SEQ, LANES), lambda b, h: (b, h)),
        out_shape=jax.ShapeDtypeStruct((m, ATTN_OUT), BF16),
        scratch_shapes=[pltpu.VMEM((len(ATTN_GROUPS), SEQ, LANES), F32)] * 2,
        compiler_params=_params("parallel", "parallel"),
        name="dilated_attn",
    )(*[a for g in qkv for a in (g, g, g)])


MIX_GATE_COLS = 512


def _mix_kernel(*refs):
    nc = D_MODEL // MIX_GATE_COLS
    oa_ref, ob_ref = refs[0], refs[1]
    ga_refs, gb_refs = refs[2:2 + nc], refs[2 + nc:2 + 2 * nc]
    x_ref, wpa_ref, wpb_ref, wo_ref, n2_ref, h_ref, v_ref = refs[2 + 2 * nc:]
    oa, ob = oa_ref[...], ob_ref[...]
    parts = []
    for c in range(nc):
        cols = slice(c * MIX_GATE_COLS, (c + 1) * MIX_GATE_COLS)
        ya = _dot(oa, wpa_ref[:, cols])
        yb = _dot(ob, wpb_ref[:, cols])
        mix = jax.nn.sigmoid(ga_refs[c][...]) * ya + jax.nn.sigmoid(gb_refs[c][...]) * yb
        parts.append(mix.astype(BF16))
    h = x_ref[...] + _dot(jnp.concatenate(parts, axis=1), wo_ref[...])
    h_ref[...] = h
    v_ref[...] = _rmsnorm_rows(h, n2_ref[...]).astype(v_ref.dtype)


def _mix(oa, ob, proj, x2, wpa, wpb, wo, norm2_g, tm=256):
    m, d = x2.shape
    gc = MIX_GATE_COLS
    nc = d // gc
    resident = dict(pipeline_mode=pl.Buffered(1))

    def gate(off, c):
        return pl.BlockSpec((tm, gc), lambda i, o=off * LANES // gc + c: (i, o))

    return pl.pallas_call(
        _mix_kernel,
        grid=(m // tm,),
        in_specs=[
            pl.BlockSpec((tm, HG_WIDTH), lambda i: (i, 0)),
            pl.BlockSpec((tm, ATTN_OUT), lambda i: (i, 0)),
            *[gate(_OFF_GA, c) for c in range(nc)],
            *[gate(_OFF_GB, c) for c in range(nc)],
            pl.BlockSpec((tm, d), lambda i: (i, 0)),
            pl.BlockSpec(wpa.shape, lambda i: (0, 0), **resident),
            pl.BlockSpec(wpb.shape, lambda i: (0, 0), **resident),
            pl.BlockSpec(wo.shape, lambda i: (0, 0), **resident),
            pl.BlockSpec((1, d), lambda i: (0, 0)),
        ],
        out_specs=[pl.BlockSpec((tm, d), lambda i: (i, 0)), pl.BlockSpec((tm, d), lambda i: (i, 0))],
        out_shape=[jax.ShapeDtypeStruct((m, d), F32), jax.ShapeDtypeStruct((m, d), BF16)],
        compiler_params=_params("parallel"),
        name="mix_outproj",
    )(oa, ob, *([proj] * (2 * nc)), x2, wpa, wpb, wo, norm2_g.reshape(1, d))


def _shift_rows(h, prev, k):
    r = pltpu.roll(h, k, axis=0)
    p = pltpu.roll(prev, k, axis=0)
    row8 = lax.broadcasted_iota(jnp.int32, prev.shape, 0)
    top = jnp.where(row8 < k, p, r[:8])
    return jnp.concatenate([top, r[8:]], axis=0)


def _causal_conv3(h, prev, w, b):
    return w[2:3] * h + w[1:2] * _shift_rows(h, prev, 1) + w[0:1] * _shift_rows(h, prev, 2) + b


def _gelu_tanh_gate(x, y):
    c = math.sqrt(2.0 / math.pi)
    w = x * ((-2.0 * c * LOG2E) + (-2.0 * c * 0.044715 * LOG2E) * (x * x))
    return (x * y) * (1.0 / (1.0 + jnp.exp2(w)))


def _ffn_up_kernel(v_ref, halo_ref, wa_ref, wb_ref, cwa_ref, cwb_ref, cba_ref, cbb_ref, o_ref, *, sub):
    tm = v_ref.shape[0]
    hr = halo_ref.shape[0]
    starts_sequence = (pl.program_id(0) * tm) % SEQ == 0
    halo = jnp.where(starts_sequence, jnp.zeros_like(halo_ref[...]), halo_ref[...])
    wa, wb = wa_ref[...].astype(BF16), wb_ref[...].astype(BF16)
    cwa, cwb, cba, cbb = cwa_ref[...], cwb_ref[...], cba_ref[...], cbb_ref[...]
    pa = pb = None
    for r0 in range(0, tm, sub):
        vs = v_ref[r0:r0 + sub, :]
        if r0 == 0:
            vs = jnp.concatenate([halo, vs], axis=0)
        ha, hb = _dot(vs, wa), _dot(vs, wb)
        if r0 == 0:
            pa, pb = ha[hr - 8:hr], hb[hr - 8:hr]
            ha, hb = ha[hr:], hb[hr:]
        ca = _causal_conv3(ha, pa, cwa, cba)
        cb = _causal_conv3(hb, pb, cwb, cbb)
        o_ref[r0:r0 + sub, :] = _gelu_tanh_gate(ca, cb).astype(o_ref.dtype)
        pa, pb = ha[sub - 8:], hb[sub - 8:]


def _ffn_up(v, w_up, conv_w, conv_b, tm=2048, tn=512, sub=512):
    m, d = v.shape
    nj = D_FF // tn
    hb = tm // BF16_SUBLANES
    return pl.pallas_call(
        functools.partial(_ffn_up_kernel, sub=sub),
        grid=(m // tm, nj),
        in_specs=[
            pl.BlockSpec((tm, d), lambda i, j: (i, 0)),
            pl.BlockSpec((BF16_SUBLANES, d), lambda i, j: (jnp.maximum(i * hb - 1, 0), 0)),
            pl.BlockSpec((d, tn), lambda i, j: (0, j)),
            pl.BlockSpec((d, tn), lambda i, j: (0, j + nj)),
            pl.BlockSpec((3, tn), lambda i, j: (0, j)),
            pl.BlockSpec((3, tn), lambda i, j: (0, j + nj)),
            pl.BlockSpec((1, tn), lambda i, j: (0, j)),
            pl.BlockSpec((1, tn), lambda i, j: (0, j + nj)),
        ],
        out_specs=pl.BlockSpec((tm, tn), lambda i, j: (i, j)),
        out_shape=jax.ShapeDtypeStruct((m, D_FF), BF16),
        compiler_params=_params("parallel", "parallel"),
        name="ffn_up_conv_gate",
    )(v, v, w_up, w_up, conv_w, conv_w, conv_b.reshape(1, -1), conv_b.reshape(1, -1))


def _ffn_down_kernel(a_ref, w_ref, h_ref, g_ref, o_ref):
    o_ref[...] = _rmsnorm_rows(h_ref[...] + _dot(a_ref[...], w_ref[...]), g_ref[...])


def _ffn_down(act, w_down, h1, norm_g, tm=512):
    m, kdim = act.shape
    d = w_down.shape[1]
    return pl.pallas_call(
        _ffn_down_kernel,
        grid=(m // tm,),
        in_specs=[
            pl.BlockSpec((tm, kdim), lambda i: (i, 0)),
            pl.BlockSpec((kdim, d), lambda i: (0, 0), pipeline_mode=pl.Buffered(1)),
            pl.BlockSpec((tm, d), lambda i: (i, 0)),
            pl.BlockSpec((1, d), lambda i: (0, 0)),
        ],
        out_specs=pl.BlockSpec((tm, d), lambda i: (i, 0)),
        out_shape=jax.ShapeDtypeStruct((m, d), F32),
        compiler_params=_params("parallel"),
        name="ffn_down_norm",
    )(act, w_down, h1, norm_g.reshape(1, d))


def kernel(x, norm1_g, w_in, hg_lb_param, hg_norm_g, w_proj_a, w_proj_b, w_out,
           norm2_g, w_up, conv_w, conv_b, w_down, norm_f_g):
    batch, seq, d = x.shape
    depth = w_in.shape[0]
    assert (seq, d, depth) == (SEQ, D_MODEL, 1) and w_in.shape[2] == IN_COLS
    h = x.reshape(batch * seq, d)
    for l in range(depth):
        u = _norm_bf16(h, norm1_g[l])
        proj, (wpa, wpb, wo, wdn) = _in_proj(u, w_in[l], (w_proj_a[l], w_proj_b[l], w_out[l], w_down[l]))
        oa = _hgrn2(proj, hg_lb_param, hg_norm_g[l], batch)
        ob = _attention([_attn_proj(u, w_in[l], g) for g in range(len(ATTN_GROUPS))], batch)
        h, v = _mix(oa, ob, proj, h, wpa, wpb, wo, norm2_g[l])
        act = _ffn_up(v, w_up[l], conv_w[l], conv_b[l])
        out = _ffn_down(act, wdn, h, norm_f_g)
    return out.reshape(batch, seq, d)
```

```python
import functools
import math

import numpy as np
import jax
import jax.numpy as jnp
from jax import lax
from jax.experimental import pallas as pl
from jax.experimental.pallas import tpu as pltpu

D_MODEL = 2048
SEQ = 2048
HG_HEADS = 16
HG_DIM = 128
HG_WIDTH = HG_HEADS * HG_DIM
ATTN_GROUPS = ((128, 1), (512, 4), (2048, 16))
ATTN_HEADS_PER_GROUP = 4
HEAD_DIM = 128
ATTN_WIDTH = len(ATTN_GROUPS) * ATTN_HEADS_PER_GROUP * HEAD_DIM
ATTN_OUT = ATTN_HEADS_PER_GROUP * HEAD_DIM
ROPE_THETA = 10000.0
D_FF = 5632
NORM_EPS = 1e-6
NEG_INF = -1e30
IN_COLS = 4 * HG_WIDTH + 3 * ATTN_WIDTH + 2 * D_MODEL

_OFF_HQ = 0
_OFF_HF = HG_WIDTH // 128
_OFF_HI = 2 * HG_WIDTH // 128
_OFF_HG = 3 * HG_WIDTH // 128
_OFF_AQ = 4 * HG_WIDTH // 128
_OFF_AK = _OFF_AQ + ATTN_WIDTH // 128
_OFF_AV = _OFF_AK + ATTN_WIDTH // 128
_OFF_GA = 4 * HG_WIDTH // 128
_OFF_GB = _OFF_GA + D_MODEL // 128

LANES = 128
BF16_SUBLANES = 16
VMEM_LIMIT = 56 * 1024 * 1024
LOG2E = 1.4426950408889634

F32 = jnp.float32
BF16 = jnp.bfloat16


def _dot(a, b):
    return jnp.dot(a, b, preferred_element_type=F32)


def _dot_nt(a, b):
    return lax.dot_general(a, b, (((1,), (1,)), ((), ())), preferred_element_type=F32)


def _dot_tn(a, b):
    return lax.dot_general(a, b, (((0,), (0,)), ((), ())), preferred_element_type=F32)


def _rmsnorm_rows(x, g):
    ms = jnp.mean(x * x, axis=-1, keepdims=True)
    return x * lax.rsqrt(ms + NORM_EPS) * g


def _sigmoid(x):
    return 1.0 / (1.0 + jnp.exp2(x * (-LOG2E)))


def _params(*sem):
    return pltpu.CompilerParams(dimension_semantics=sem, vmem_limit_bytes=VMEM_LIMIT)


def _norm_kernel(x_ref, g_ref, u_ref):
    u_ref[...] = _rmsnorm_rows(x_ref[...], g_ref[...]).astype(u_ref.dtype)


def _norm_bf16(x2, norm_g, tm=1024):
    m, d = x2.shape
    return pl.pallas_call(
        _norm_kernel,
        grid=(m // tm,),
        in_specs=[pl.BlockSpec((tm, d), lambda i: (i, 0)), pl.BlockSpec((1, d), lambda i: (0, 0))],
        out_specs=pl.BlockSpec((tm, d), lambda i: (i, 0)),
        out_shape=jax.ShapeDtypeStruct((m, d), BF16),
        compiler_params=_params("parallel"),
        name="norm1",
    )(x2, norm_g.reshape(1, d))


def _inproj_kernel(u_ref, w_ref, *refs):
    n_side = (len(refs) - 1) // 2
    o_ref = refs[n_side]
    o_ref[...] = _dot(u_ref[...], w_ref[...].astype(BF16))
    for src, dst in zip(refs[:n_side], refs[n_side + 1:]):
        dst[...] = src[...].astype(dst.dtype)


def _in_proj(u, w, side_weights, tm=4096, tn=512):
    m, d = u.shape
    n_hg, skip = 4 * HG_WIDTH // tn, 3 * ATTN_WIDTH // tn
    nj = n_hg + 2 * D_MODEL // tn
    steps = (m // tm) * nj
    side_in, side_out, side_shape = [], [], []
    for sw in side_weights:
        rows = -(-sw.shape[0] // steps)
        rows = -(-rows // BF16_SUBLANES) * BF16_SUBLANES
        while sw.shape[0] % rows:
            rows += BF16_SUBLANES
        last = sw.shape[0] // rows - 1
        spec = pl.BlockSpec((rows, sw.shape[1]), lambda i, j, last=last: (jnp.minimum(i * nj + j, last), 0))
        side_in.append(spec)
        side_out.append(spec)
        side_shape.append(jax.ShapeDtypeStruct(sw.shape, BF16))
    outs = pl.pallas_call(
        _inproj_kernel,
        grid=(m // tm, nj),
        in_specs=[
            pl.BlockSpec((tm, d), lambda i, j: (i, 0), pipeline_mode=pl.Buffered(1)),
            pl.BlockSpec((d, tn), lambda i, j: (0, jnp.where(j < n_hg, j, j + skip))),
            *side_in,
        ],
        out_specs=[pl.BlockSpec((tm, tn), lambda i, j: (i, j)), *side_out],
        out_shape=[jax.ShapeDtypeStruct((m, nj * tn), F32), *side_shape],
        compiler_params=_params("arbitrary", "arbitrary"),
        name="in_proj",
    )(u, w, *side_weights)
    return outs[0], outs[1:]


def _attn_proj_kernel(u_ref, w_ref, cos_ref, sin_ref, o_ref, scr, scr2, *, dil, sub):
    w = w_ref[...].astype(BF16)
    tm = u_ref.shape[0]
    length, n = tm // dil, sub // dil
    for r0 in range(0, tm, sub):
        acc = _dot(u_ref[r0:r0 + sub, :], w)
        cs, sn = cos_ref[r0:r0 + sub, :], sin_ref[r0:r0 + sub, :]
        for hh in range(acc.shape[1] // HEAD_DIM):
            ln = slice(hh * HEAD_DIM, (hh + 1) * HEAD_DIM)
            x = acc[:, ln]
            y = x * cs + pltpu.roll(x, HEAD_DIM // 2, axis=1) * sn
            if dil == 1:
                o_ref[r0:r0 + sub, ln] = y.astype(o_ref.dtype)
                continue
            scr[hh] = y
            for r1 in range(4):
                part = scr[hh, pl.ds(r1, sub // 4, stride=4), :]
                if dil == 4:
                    dst = r1 * length + r0 // dil
                    o_ref[dst:dst + n, ln] = part.astype(o_ref.dtype)
                    continue
                scr2[hh, r1] = part
                for r2 in range(4):
                    dst = _attn_slot(r1 + 4 * r2) * length + r0 // dil
                    o_ref[dst:dst + n, ln] = scr2[hh, r1, pl.ds(r2, n, stride=4), :].astype(o_ref.dtype)


def _attn_slot(r):
    return (r % 4) * 4 + r // 4


def _attn_proj(u, w, group, tm=SEQ, sub=512):
    m, d = u.shape
    dil = ATTN_GROUPS[group][1]
    assert dil in (1, 4, 16)
    tn = ATTN_HEADS_PER_GROUP * HEAD_DIM
    half = HEAD_DIM // 2
    inv_freq = jnp.exp(-math.log(ROPE_THETA) * jnp.arange(half, dtype=F32) * (2.0 / HEAD_DIM))
    ang = jnp.arange(SEQ, dtype=F32)[:, None] * inv_freq[None, :]
    cos, sin = jnp.cos(ang), jnp.sin(ang)
    one, zero = jnp.ones((SEQ, HEAD_DIM), F32), jnp.zeros((SEQ, HEAD_DIM), F32)
    cos_t = jnp.stack([jnp.concatenate([cos, cos], axis=1), one])
    sin_t = jnp.stack([jnp.concatenate([-sin, sin], axis=1), zero])
    first = (_OFF_AQ * LANES) // tn + group
    per = ATTN_WIDTH // tn
    table = pl.BlockSpec((None, SEQ, HEAD_DIM), lambda i, j: (j // 2, 0, 0))
    return pl.pallas_call(
        functools.partial(_attn_proj_kernel, dil=dil, sub=sub),
        grid=(m // tm, 3),
        in_specs=[
            pl.BlockSpec((tm, d), lambda i, j: (i, 0)),
            pl.BlockSpec((d, tn), lambda i, j: (0, first + per * j)),
            table, table,
        ],
        out_specs=pl.BlockSpec((tm, tn), lambda i, j: (i, j)),
        out_shape=jax.ShapeDtypeStruct((m, 3 * tn), BF16),
        scratch_shapes=[pltpu.VMEM((tn // HEAD_DIM, sub, HEAD_DIM), F32),
                        pltpu.VMEM((tn // HEAD_DIM, 4, sub // 4, HEAD_DIM), F32)],
        compiler_params=_params("parallel", "parallel"),
        name="attn_proj",
    )(u, w, cos_t, sin_t)


HG_CHUNK = 128
HG_HEADS_PER_STEP = 2
HG_CHUNKS_PER_STEP = 8


def _hgrn_tables(c):
    t = np.arange(c)[:, None]
    s = np.arange(c)[None, :]
    x = t ^ s
    lvl = np.where(x > 0, np.floor(np.log2(np.maximum(x, 1))), -1).astype(np.int32)
    lvl = np.where(s > t, -2, lvl).astype(np.int32)
    tri = (s <= t).astype(np.float32)
    return jnp.asarray(tri, BF16), jnp.asarray(lvl, jnp.int32)


def _neg_abs(d):
    bits = lax.bitcast_convert_type(d, jnp.uint32) | jnp.uint32(0x80000000)
    return lax.bitcast_convert_type(bits, F32)


def _hgrn_chunk(zq, zf, vv, zg, sts, lb, oml, ng, tri, lvl, row):
    heads = range(len(zq))
    c = zq[0].shape[0]
    f = [lb[i] + oml[i] * _sigmoid(zf[i]) for i in heads]
    kk = [1.0 - f[i] for i in heads]
    qq = [zq[i] * _sigmoid(zq[i]) for i in heads]
    gl = [jnp.log(f[i]) * LOG2E for i in heads]

    gcum = []
    for i in heads:
        g1 = gl[i].astype(BF16)
        g2 = (gl[i] - g1.astype(F32)).astype(BF16)
        gc = _dot(tri, jnp.concatenate([g1, g2], axis=1))
        gcum.append(gc[:, :LANES] + gc[:, LANES:])
    glast = [gcum[i][c - 1:c, :] for i in heads]

    qb = [qq[i].astype(BF16) for i in heads]
    kb = [kk[i].astype(BF16) for i in heads]
    st_in, a = list(sts), []
    for i in heads:
        kd = (kk[i] * jnp.exp2(glast[i] - gcum[i])).astype(BF16)
        st_in.append(st_in[i] * jnp.exp2(glast[i]) + _dot_tn(vv[i], kd))
        a.append(jnp.where(lvl == -1, _dot_nt(qb[i], kb[i]), 0.0))

    h = c // 2
    while h >= 1:
        in_level = lvl == int(math.log2(h))
        for i in heads:
            if h >= 4:
                g3d = gcum[i].reshape(c // (2 * h), 2 * h, LANES)
                ref = jnp.broadcast_to(g3d[:, h - 1:h, :], g3d.shape).reshape(c, LANES)
                x = _neg_abs(gcum[i] - ref)
            elif h == 2:
                pos = row & 3
                up = pltpu.roll(gl[i], c - 1, axis=0)
                dn = pltpu.roll(gl[i], 1, axis=0)
                x = jnp.where(pos == 0, up, jnp.where(pos == 1, 0.0, jnp.where(pos == 2, gl[i], gl[i] + dn)))
            else:
                x = jnp.where((row & 1) == 1, gl[i], 0.0)
            e = jnp.exp2(x).astype(BF16)
            a[i] = jnp.where(in_level, _dot_nt(qb[i] * e, kb[i] * e), a[i])
        h //= 2
    o = []
    for i in heads:
        qg = (qq[i] * jnp.exp2(gcum[i])).astype(BF16)
        o.append(_dot_nt(qg, st_in[i].astype(BF16)) + _dot(a[i].astype(BF16), vv[i]))
    y = [_rmsnorm_rows(o[i], ng) * (zg[i] * _sigmoid(zg[i])) for i in heads]
    return y, st_in[len(zq):]


def _hgrn_kernel(q_ref, f_ref, i_ref, g_ref, lbp_ref, ng_ref, tri_ref, lvl_ref, o_ref, *, chunk, heads):
    c = chunk
    seq = q_ref.shape[0]
    p = lbp_ref[...]
    pe = jnp.exp(p - jnp.max(p, axis=0, keepdims=True))
    lb = pe[0:1, :] / jnp.sum(pe, axis=0, keepdims=True)
    oml = 1.0 - lb
    ng = ng_ref[...]
    row = lax.broadcasted_iota(jnp.int32, (c, LANES), 0)
    lanes = [slice(hh * LANES, (hh + 1) * LANES) for hh in range(heads)]
    nch = HG_CHUNKS_PER_STEP

    def body(ci, sts):
        ent = [(pl.ds(pl.multiple_of((ci * nch + ch) * c, c), c), ln) for ch in range(nch) for ln in lanes]
        y, new = _hgrn_chunk(
            [q_ref[rows, ln] for rows, ln in ent], [f_ref[rows, ln] for rows, ln in ent],
            [i_ref[rows, ln].astype(BF16) for rows, ln in ent], [g_ref[rows, ln] for rows, ln in ent],
            sts, [lb[:, ln] for _, ln in ent], [oml[:, ln] for _, ln in ent],
            ng, tri_ref[...], lvl_ref[...], row)
        for (rows, ln), yy in zip(ent, y):
            o_ref[rows, ln] = yy.astype(o_ref.dtype)
        return tuple(new)

    lax.fori_loop(0, seq // (c * nch), body, tuple(jnp.zeros((HG_DIM, HG_DIM), F32) for _ in range(heads)))


def _hgrn2(proj, lb_param, norm_g, batch):
    c, hp = HG_CHUNK, HG_HEADS_PER_STEP
    tri, lvl = _hgrn_tables(c)
    m = proj.shape[0]
    w = hp * LANES

    def col(off):
        return pl.BlockSpec((SEQ, w), lambda b, h, o=off // hp: (b, o + h))

    return pl.pallas_call(
        functools.partial(_hgrn_kernel, chunk=c, heads=hp),
        grid=(batch, HG_HEADS // hp),
        in_specs=[
            col(_OFF_HQ), col(_OFF_HF), col(_OFF_HI), col(_OFF_HG),
            pl.BlockSpec((lb_param.shape[0], w), lambda b, h: (0, h)),
            pl.BlockSpec((1, LANES), lambda b, h: (0, 0)),
            pl.BlockSpec((c, c), lambda b, h: (0, 0)),
            pl.BlockSpec((c, c), lambda b, h: (0, 0)),
        ],
        out_specs=pl.BlockSpec((SEQ, w), lambda b, h: (b, h)),
        out_shape=jax.ShapeDtypeStruct((m, HG_WIDTH), BF16),
        compiler_params=_params("parallel", "parallel"),
        name="hgrn2",
    )(proj, proj, proj, proj, lb_param, norm_g.reshape(1, LANES), tri, lvl)


ATTN_BLOCK = 128
ATTN_BATCH = 8


def _attn_kernel(q0, k0, v0, q1, k1, v1, q2, k2, v2, o_ref, og_ref, lse_ref):
    blk = ATTN_BLOCK
    scale = HEAD_DIM ** -0.5
    qi = lax.broadcasted_iota(jnp.int32, (blk, 2 * blk), 0)
    ki = lax.broadcasted_iota(jnp.int32, (blk, 2 * blk), 1)
    mask2 = (ki >= qi) & (ki <= qi + blk)
    mask1 = (lax.broadcasted_iota(jnp.int32, (blk, blk), 1)
             <= lax.broadcasted_iota(jnp.int32, (blk, blk), 0))

    def run_batch(items):
        s = [_dot_nt(qj, kw) * (scale * LOG2E) for qj, kw, _, _, _ in items]
        ps, mxs = [], []
        for (_, _, _, mask, _), sj in zip(items, s):
            sj = jnp.where(mask, sj, NEG_INF)
            mx = jnp.max(sj, axis=-1, keepdims=True)
            ps.append(jnp.exp2(sj - mx).astype(BF16))
            mxs.append(mx)
        os = [_dot(p, jnp.concatenate([vw, jnp.ones_like(vw)], axis=1)) for p, (_, _, vw, _, _) in zip(ps, items)]
        for (_, _, _, _, (g, rows)), o, mx in zip(items, os, mxs):
            den = o[:, HEAD_DIM:]
            og_ref.at[g][rows, :] = o[:, :HEAD_DIM] * (1.0 / den)
            lse_ref.at[g][rows, :] = mx + jnp.log(den) * LOG2E

    refs = ((q0, k0, v0), (q1, k1, v1), (q2, k2, v2))
    pending = []
    for g, (window, dil) in enumerate(ATTN_GROUPS):
        assert window // dil == blk
        q_ref, k_ref, v_ref = refs[g]
        length = SEQ // dil
        for r in range(dil):
            base = (_attn_slot(r) if dil == 16 else r) * length
            for j in range(length // blk):
                lo, hi = base + max(j - 1, 0) * blk, base + (j + 1) * blk
                rows = pl.ds(j * blk, blk) if dil == 1 else pl.ds(j * blk * dil + r, blk, stride=dil)
                pending.append((q_ref[hi - blk:hi, :], k_ref[lo:hi, :], v_ref[lo:hi, :],
                                mask1 if j == 0 else mask2, (g, rows)))
                if len(pending) == ATTN_BATCH:
                    run_batch(pending)
                    pending = []
    assert not pending

    l0, l1, l2 = lse_ref[0], lse_ref[1], lse_ref[2]
    mx = jnp.maximum(jnp.maximum(l0, l1), l2)
    w0, w1, w2 = jnp.exp2(l0 - mx), jnp.exp2(l1 - mx), jnp.exp2(l2 - mx)
    o = (w0 * og_ref[0] + w1 * og_ref[1] + w2 * og_ref[2]) / (w0 + w1 + w2)
    o_ref[...] = o.astype(o_ref.dtype)


def _attention(qkv, batch):
    m = qkv[0].shape[0]
    hg = ATTN_HEADS_PER_GROUP

    def col(part):
        return pl.BlockSpec((SEQ, LANES), lambda b, h, o=part * hg: (b, o + h))

    return pl.pallas_call(
        _attn_kernel,
        grid=(batch, hg),
        in_specs=[col(0), col(1), col(2)] * len(ATTN_GROUPS),
        out_specs=pl.BlockSpec((SEQ, LANES), lambda b, h: (b, h)),
        out_shape=jax.ShapeDtypeStruct((m, ATTN_OUT), BF16),
        scratch_shapes=[pltpu.VMEM((len(ATTN_GROUPS), SEQ, LANES), F32)] * 2,
        compiler_params=_params("parallel", "parallel"),
        name="dilated_attn",
    )(*[a for g in qkv for a in (g, g, g)])


MIX_GATE_COLS = 512


def _mix_kernel(*refs):
    nc = D_MODEL // MIX_GATE_COLS
    oa_ref, ob_ref = refs[0], refs[1]
    ga_refs, gb_refs = refs[2:2 + nc], refs[2 + nc:2 + 2 * nc]
    x_ref, wpa_ref, wpb_ref, wo_ref, n2_ref, h_ref, v_ref = refs[2 + 2 * nc:]
    oa, ob = oa_ref[...], ob_ref[...]
    parts = []
    for c in range(nc):
        cols = slice(c * MIX_GATE_COLS, (c + 1) * MIX_GATE_COLS)
        ya = _dot(oa, wpa_ref[:, cols])
        yb = _dot(ob, wpb_ref[:, cols])
        mix = jax.nn.sigmoid(ga_refs[c][...]) * ya + jax.nn.sigmoid(gb_refs[c][...]) * yb
        parts.append(mix.astype(BF16))
    h = x_ref[...] + _dot(jnp.concatenate(parts, axis=1), wo_ref[...])
    h_ref[...] = h
    v_ref[...] = _rmsnorm_rows(h, n2_ref[...]).astype(v_ref.dtype)


def _mix(oa, ob, proj, x2, wpa, wpb, wo, norm2_g, tm=256):
    m, d = x2.shape
    gc = MIX_GATE_COLS
    nc = d // gc
    resident = dict(pipeline_mode=pl.Buffered(1))

    def gate(off, c):
        return pl.BlockSpec((tm, gc), lambda i, o=off * LANES // gc + c: (i, o))

    return pl.pallas_call(
        _mix_kernel,
        grid=(m // tm,),
        in_specs=[
            pl.BlockSpec((tm, HG_WIDTH), lambda i: (i, 0)),
            pl.BlockSpec((tm, ATTN_OUT), lambda i: (i, 0)),
            *[gate(_OFF_GA, c) for c in range(nc)],
            *[gate(_OFF_GB, c) for c in range(nc)],
            pl.BlockSpec((tm, d), lambda i: (i, 0)),
            pl.BlockSpec(wpa.shape, lambda i: (0, 0), **resident),
            pl.BlockSpec(wpb.shape, lambda i: (0, 0), **resident),
            pl.BlockSpec(wo.shape, lambda i: (0, 0), **resident),
            pl.BlockSpec((1, d), lambda i: (0, 0)),
        ],
        out_specs=[pl.BlockSpec((tm, d), lambda i: (i, 0)), pl.BlockSpec((tm, d), lambda i: (i, 0))],
        out_shape=[jax.ShapeDtypeStruct((m, d), F32), jax.ShapeDtypeStruct((m, d), BF16)],
        compiler_params=_params("parallel"),
        name="mix_outproj",
    )(oa, ob, *([proj] * (2 * nc)), x2, wpa, wpb, wo, norm2_g.reshape(1, d))


def _shift_rows(h, prev, k):
    r = pltpu.roll(h, k, axis=0)
    p = pltpu.roll(prev, k, axis=0)
    row8 = lax.broadcasted_iota(jnp.int32, prev.shape, 0)
    top = jnp.where(row8 < k, p, r[:8])
    return jnp.concatenate([top, r[8:]], axis=0)


def _causal_conv3(h, prev, w, b):
    return w[2:3] * h + w[1:2] * _shift_rows(h, prev, 1) + w[0:1] * _shift_rows(h, prev, 2) + b


def _gelu_tanh_gate(x, y):
    c = math.sqrt(2.0 / math.pi)
    w = x * ((-2.0 * c * LOG2E) + (-2.0 * c * 0.044715 * LOG2E) * (x * x))
    return (x * y) * (1.0 / (1.0 + jnp.exp2(w)))


def _ffn_up_kernel(v_ref, halo_ref, wa_ref, wb_ref, cwa_ref, cwb_ref, cba_ref, cbb_ref, o_ref, *, sub):
    tm = v_ref.shape[0]
    hr = halo_ref.shape[0]
    starts_sequence = (pl.program_id(0) * tm) % SEQ == 0
    halo = jnp.where(starts_sequence, jnp.zeros_like(halo_ref[...]), halo_ref[...])
    wa, wb = wa_ref[...].astype(BF16), wb_ref[...].astype(BF16)
    cwa, cwb, cba, cbb = cwa_ref[...], cwb_ref[...], cba_ref[...], cbb_ref[...]
    pa = pb = None
    for r0 in range(0, tm, sub):
        vs = v_ref[r0:r0 + sub, :]
        if r0 == 0:
            vs = jnp.concatenate([halo, vs], axis=0)
        ha, hb = _dot(vs, wa), _dot(vs, wb)
        if r0 == 0:
            pa, pb = ha[hr - 8:hr], hb[hr - 8:hr]
            ha, hb = ha[hr:], hb[hr:]
        ca = _causal_conv3(ha, pa, cwa, cba)
        cb = _causal_conv3(hb, pb, cwb, cbb)
        o_ref[r0:r0 + sub, :] = _gelu_tanh_gate(ca, cb).astype(o_ref.dtype)
        pa, pb = ha[sub - 8:], hb[sub - 8:]


def _ffn_up(v, w_up, conv_w, conv_b, tm=2048, tn=512, sub=512):
    m, d = v.shape
    nj = D_FF // tn
    hb = tm // BF16_SUBLANES
    return pl.pallas_call(
        functools.partial(_ffn_up_kernel, sub=sub),
        grid=(m // tm, nj),
        in_specs=[
            pl.BlockSpec((tm, d), lambda i, j: (i, 0)),
            pl.BlockSpec((BF16_SUBLANES, d), lambda i, j: (jnp.maximum(i * hb - 1, 0), 0)),
            pl.BlockSpec((d, tn), lambda i, j: (0, j)),
            pl.BlockSpec((d, tn), lambda i, j: (0, j + nj)),
            pl.BlockSpec((3, tn), lambda i, j: (0, j)),
            pl.BlockSpec((3, tn), lambda i, j: (0, j + nj)),
            pl.BlockSpec((1, tn), lambda i, j: (0, j)),
            pl.BlockSpec((1, tn), lambda i, j: (0, j + nj)),
        ],
        out_specs=pl.BlockSpec((tm, tn), lambda i, j: (i, j)),
        out_shape=jax.ShapeDtypeStruct((m, D_FF), BF16),
        compiler_params=_params("parallel", "parallel"),
        name="ffn_up_conv_gate",
    )(v, v, w_up, w_up, conv_w, conv_w, conv_b.reshape(1, -1), conv_b.reshape(1, -1))


def _ffn_down_kernel(a_ref, w_ref, h_ref, g_ref, o_ref):
    o_ref[...] = _rmsnorm_rows(h_ref[...] + _dot(a_ref[...], w_ref[...]), g_ref[...])


def _ffn_down(act, w_down, h1, norm_g, tm=512):
    m, kdim = act.shape
    d = w_down.shape[1]
    return pl.pallas_call(
        _ffn_down_kernel,
        grid=(m // tm,),
        in_specs=[
            pl.BlockSpec((tm, kdim), lambda i: (i, 0)),
            pl.BlockSpec((kdim, d), lambda i: (0, 0), pipeline_mode=pl.Buffered(1)),
            pl.BlockSpec((tm, d), lambda i: (i, 0)),
            pl.BlockSpec((1, d), lambda i: (0, 0)),
        ],
        out_specs=pl.BlockSpec((tm, d), lambda i: (i, 0)),
        out_shape=jax.ShapeDtypeStruct((m, d), F32),
        compiler_params=_params("parallel"),
        name="ffn_down_norm",
    )(act, w_down, h1, norm_g.reshape(1, d))


def kernel(x, norm1_g, w_in, hg_lb_param, hg_norm_g, w_proj_a, w_proj_b, w_out,
           norm2_g, w_up, conv_w, conv_b, w_down, norm_f_g):
    batch, seq, d = x.shape
    depth = w_in.shape[0]
    assert (seq, d, depth) == (SEQ, D_MODEL, 1) and w_in.shape[2] == IN_COLS
    h = x.reshape(batch * seq, d)
    for l in range(depth):
        u = _norm_bf16(h, norm1_g[l])
        proj, (wpa, wpb, wo, wdn) = _in_proj(u, w_in[l], (w_proj_a[l], w_proj_b[l], w_out[l], w_down[l]))
        oa = _hgrn2(proj, hg_lb_param, hg_norm_g[l], batch)
        ob = _attention([_attn_proj(u, w_in[l], g) for g in range(len(ATTN_GROUPS))], batch)
        h, v = _mix(oa, ob, proj, h, wpa, wpb, wo, norm2_g[l])
        act = _ffn_up(v, w_up[l], conv_w[l], conv_b[l])
        out = _ffn_down(act, wdn, h, norm_f_g)
    return out.reshape(batch, seq, d)
```

```python
import functools
import math

import numpy as np
import jax
import jax.numpy as jnp
from jax import lax
from jax.experimental import pallas as pl
from jax.experimental.pallas import tpu as pltpu

D_MODEL = 2048
SEQ = 2048
HG_HEADS = 16
HG_DIM = 128
HG_WIDTH = HG_HEADS * HG_DIM
ATTN_GROUPS = ((128, 1), (512, 4), (2048, 16))
ATTN_HEADS_PER_GROUP = 4
HEAD_DIM = 128
ATTN_WIDTH = len(ATTN_GROUPS) * ATTN_HEADS_PER_GROUP * HEAD_DIM
ATTN_OUT = ATTN_HEADS_PER_GROUP * HEAD_DIM
ROPE_THETA = 10000.0
D_FF = 5632
NORM_EPS = 1e-6
NEG_INF = -1e30
IN_COLS = 4 * HG_WIDTH + 3 * ATTN_WIDTH + 2 * D_MODEL

_OFF_HQ = 0
_OFF_HF = HG_WIDTH // 128
_OFF_HI = 2 * HG_WIDTH // 128
_OFF_HG = 3 * HG_WIDTH // 128
_OFF_AQ = 4 * HG_WIDTH // 128
_OFF_AK = _OFF_AQ + ATTN_WIDTH // 128
_OFF_AV = _OFF_AK + ATTN_WIDTH // 128
_OFF_GA = 4 * HG_WIDTH // 128
_OFF_GB = _OFF_GA + D_MODEL // 128

LANES = 128
BF16_SUBLANES = 16
VMEM_LIMIT = 56 * 1024 * 1024
LOG2E = 1.4426950408889634

F32 = jnp.float32
BF16 = jnp.bfloat16


def _dot(a, b):
    return jnp.dot(a, b, preferred_element_type=F32)


def _dot_nt(a, b):
    return lax.dot_general(a, b, (((1,), (1,)), ((), ())), preferred_element_type=F32)


def _dot_tn(a, b):
    return lax.dot_general(a, b, (((0,), (0,)), ((), ())), preferred_element_type=F32)


def _rmsnorm_rows(x, g):
    ms = jnp.mean(x * x, axis=-1, keepdims=True)
    return x * lax.rsqrt(ms + NORM_EPS) * g


def _sigmoid(x):
    return 1.0 / (1.0 + jnp.exp2(x * (-LOG2E)))


def _params(*sem):
    return pltpu.CompilerParams(dimension_semantics=sem, vmem_limit_bytes=VMEM_LIMIT)


def _norm_kernel(x_ref, g_ref, u_ref):
    u_ref[...] = _rmsnorm_rows(x_ref[...], g_ref[...]).astype(u_ref.dtype)


def _norm_bf16(x2, norm_g, tm=1024):
    m, d = x2.shape
    return pl.pallas_call(
        _norm_kernel,
        grid=(m // tm,),
        in_specs=[pl.BlockSpec((tm, d), lambda i: (i, 0)), pl.BlockSpec((1, d), lambda i: (0, 0))],
        out_specs=pl.BlockSpec((tm, d), lambda i: (i, 0)),
        out_shape=jax.ShapeDtypeStruct((m, d), BF16),
        compiler_params=_params("parallel"),
        name="norm1",
    )(x2, norm_g.reshape(1, d))


def _inproj_kernel(u_ref, w_ref, *refs):
    n_side = (len(refs) - 1) // 2
    o_ref = refs[n_side]
    o_ref[...] = _dot(u_ref[...], w_ref[...].astype(BF16))
    for src, dst in zip(refs[:n_side], refs[n_side + 1:]):
        dst[...] = src[...].astype(dst.dtype)


def _in_proj(u, w, side_weights, tm=4096, tn=512):
    m, d = u.shape
    n_hg, skip = 4 * HG_WIDTH // tn, 3 * ATTN_WIDTH // tn
    nj = n_hg + 2 * D_MODEL // tn
    steps = (m // tm) * nj
    side_in, side_out, side_shape = [], [], []
    for sw in side_weights:
        rows = -(-sw.shape[0] // steps)
        rows = -(-rows // BF16_SUBLANES) * BF16_SUBLANES
        while sw.shape[0] % rows:
            rows += BF16_SUBLANES
        last = sw.shape[0] // rows - 1
        spec = pl.BlockSpec((rows, sw.shape[1]), lambda i, j, last=last: (jnp.minimum(i * nj + j, last), 0))
        side_in.append(spec)
        side_out.append(spec)
        side_shape.append(jax.ShapeDtypeStruct(sw.shape, BF16))
    outs = pl.pallas_call(
        _inproj_kernel,
        grid=(m // tm, nj),
        in_specs=[
            pl.BlockSpec((tm, d), lambda i, j: (i, 0), pipeline_mode=pl.Buffered(1)),
            pl.BlockSpec((d, tn), lambda i, j: (0, jnp.where(j < n_hg, j, j + skip))),
            *side_in,
        ],
        out_specs=[pl.BlockSpec((tm, tn), lambda i, j: (i, j)), *side_out],
        out_shape=[jax.ShapeDtypeStruct((m, nj * tn), F32), *side_shape],
        compiler_params=_params("arbitrary", "arbitrary"),
        name="in_proj",
    )(u, w, *side_weights)
    return outs[0], outs[1:]


def _attn_proj_kernel(u_ref, w_ref, cos_ref, sin_ref, o_ref, scr, scr2, *, dil, sub):
    w = w_ref[...].astype(BF16)
    tm = u_ref.shape[0]
    length, n = tm // dil, sub // dil
    for r0 in range(0, tm, sub):
        acc = _dot(u_ref[r0:r0 + sub, :], w)
        cs, sn = cos_ref[r0:r0 + sub, :], sin_ref[r0:r0 + sub, :]
        for hh in range(acc.shape[1] // HEAD_DIM):
            ln = slice(hh * HEAD_DIM, (hh + 1) * HEAD_DIM)
            x = acc[:, ln]
            y = x * cs + pltpu.roll(x, HEAD_DIM // 2, axis=1) * sn
            if dil == 1:
                o_ref[r0:r0 + sub, ln] = y.astype(o_ref.dtype)
                continue
            scr[hh] = y
            for r1 in range(4):
                part = scr[hh, pl.ds(r1, sub // 4, stride=4), :]
                if dil == 4:
                    dst = r1 * length + r0 // dil
                    o_ref[dst:dst + n, ln] = part.astype(o_ref.dtype)
                    continue
                scr2[hh, r1] = part
                for r2 in range(4):
                    dst = _attn_slot(r1 + 4 * r2) * length + r0 // dil
                    o_ref[dst:dst + n, ln] = scr2[hh, r1, pl.ds(r2, n, stride=4), :].astype(o_ref.dtype)


def _attn_slot(r):
    return (r % 4) * 4 + r // 4


def _attn_proj(u, w, group, tm=SEQ, sub=1024):
    m, d = u.shape
    dil = ATTN_GROUPS[group][1]
    assert dil in (1, 4, 16)
    tn = ATTN_HEADS_PER_GROUP * HEAD_DIM
    half = HEAD_DIM // 2
    inv_freq = jnp.exp(-math.log(ROPE_THETA) * jnp.arange(half, dtype=F32) * (2.0 / HEAD_DIM))
    ang = jnp.arange(SEQ, dtype=F32)[:, None] * inv_freq[None, :]
    cos, sin = jnp.cos(ang), jnp.sin(ang)
    one, zero = jnp.ones((SEQ, HEAD_DIM), F32), jnp.zeros((SEQ, HEAD_DIM), F32)
    cos_t = jnp.stack([jnp.concatenate([cos, cos], axis=1), one])
    sin_t = jnp.stack([jnp.concatenate([-sin, sin], axis=1), zero])
    first = (_OFF_AQ * LANES) // tn + group
    per = ATTN_WIDTH // tn
    table = pl.BlockSpec((None, SEQ, HEAD_DIM), lambda i, j: (j // 2, 0, 0))
    return pl.pallas_call(
        functools.partial(_attn_proj_kernel, dil=dil, sub=sub),
        grid=(m // tm, 3),
        in_specs=[
            pl.BlockSpec((tm, d), lambda i, j: (i, 0)),
            pl.BlockSpec((d, tn), lambda i, j: (0, first + per * j)),
            table, table,
        ],
        out_specs=pl.BlockSpec((tm, tn), lambda i, j: (i, j)),
        out_shape=jax.ShapeDtypeStruct((m, 3 * tn), BF16),
        scratch_shapes=[pltpu.VMEM((tn // HEAD_DIM, sub, HEAD_DIM), F32),
                        pltpu.VMEM((tn // HEAD_DIM, 4, sub // 4, HEAD_DIM), F32)],
        compiler_params=_params("parallel", "parallel"),
        name="attn_proj",
    )(u, w, cos_t, sin_t)


HG_CHUNK = 128
HG_HEADS_PER_STEP = 2
HG_CHUNKS_PER_STEP = 8


def _hgrn_tables(c):
    t = np.arange(c)[:, None]
    s = np.arange(c)[None, :]
    x = t ^ s
    lvl = np.where(x > 0, np.floor(np.log2(np.maximum(x, 1))), -1).astype(np.int32)
    lvl = np.where(s > t, -2, lvl).astype(np.int32)
    tri = (s <= t).astype(np.float32)
    return jnp.asarray(tri, BF16), jnp.asarray(lvl, jnp.int32)


def _neg_abs(d):
    bits = lax.bitcast_convert_type(d, jnp.uint32) | jnp.uint32(0x80000000)
    return lax.bitcast_convert_type(bits, F32)


def _hgrn_chunk(zq, zf, vv, zg, sts, lb, oml, ng, tri, lvl, row):
    heads = range(len(zq))
    c = zq[0].shape[0]
    f = [lb[i] + oml[i] * _sigmoid(zf[i]) for i in heads]
    kk = [1.0 - f[i] for i in heads]
    qq = [zq[i] * _sigmoid(zq[i]) for i in heads]
    gl = [jnp.log(f[i]) * LOG2E for i in heads]

    gcum = []
    for i in heads:
        g1 = gl[i].astype(BF16)
        g2 = (gl[i] - g1.astype(F32)).astype(BF16)
        gc = _dot(tri, jnp.concatenate([g1, g2], axis=1))
        gcum.append(gc[:, :LANES] + gc[:, LANES:])
    glast = [gcum[i][c - 1:c, :] for i in heads]

    qb = [qq[i].astype(BF16) for i in heads]
    kb = [kk[i].astype(BF16) for i in heads]
    st_in, a = list(sts), []
    for i in heads:
        kd = (kk[i] * jnp.exp2(glast[i] - gcum[i])).astype(BF16)
        st_in.append(st_in[i] * jnp.exp2(glast[i]) + _dot_tn(vv[i], kd))
        a.append(jnp.where(lvl == -1, _dot_nt(qb[i], kb[i]), 0.0))

    h = c // 2
    while h >= 1:
        in_level = lvl == int(math.log2(h))
        for i in heads:
            if h >= 4:
                g3d = gcum[i].reshape(c // (2 * h), 2 * h, LANES)
                ref = jnp.broadcast_to(g3d[:, h - 1:h, :], g3d.shape).reshape(c, LANES)
                x = _neg_abs(gcum[i] - ref)
            elif h == 2:
                pos = row & 3
                up = pltpu.roll(gl[i], c - 1, axis=0)
                dn = pltpu.roll(gl[i], 1, axis=0)
                x = jnp.where(pos == 0, up, jnp.where(pos == 1, 0.0, jnp.where(pos == 2, gl[i], gl[i] + dn)))
            else:
                x = jnp.where((row & 1) == 1, gl[i], 0.0)
            e = jnp.exp2(x).astype(BF16)
            a[i] = jnp.where(in_level, _dot_nt(qb[i] * e, kb[i] * e), a[i])
        h //= 2
    o = []
    for i in heads:
        qg = (qq[i] * jnp.exp2(gcum[i])).astype(BF16)
        o.append(_dot_nt(qg, st_in[i].astype(BF16)) + _dot(a[i].astype(BF16), vv[i]))
    y = [_rmsnorm_rows(o[i], ng) * (zg[i] * _sigmoid(zg[i])) for i in heads]
    return y, st_in[len(zq):]


def _hgrn_kernel(q_ref, f_ref, i_ref, g_ref, lbp_ref, ng_ref, tri_ref, lvl_ref, o_ref, *, chunk, heads):
    c = chunk
    seq = q_ref.shape[0]
    p = lbp_ref[...]
    pe = jnp.exp(p - jnp.max(p, axis=0, keepdims=True))
    lb = pe[0:1, :] / jnp.sum(pe, axis=0, keepdims=True)
    oml = 1.0 - lb
    ng = ng_ref[...]
    row = lax.broadcasted_iota(jnp.int32, (c, LANES), 0)
    lanes = [slice(hh * LANES, (hh + 1) * LANES) for hh in range(heads)]
    nch = HG_CHUNKS_PER_STEP

    def body(ci, sts):
        ent = [(pl.ds(pl.multiple_of((ci * nch + ch) * c, c), c), ln) for ch in range(nch) for ln in lanes]
        y, new = _hgrn_chunk(
            [q_ref[rows, ln] for rows, ln in ent], [f_ref[rows, ln] for rows, ln in ent],
            [i_ref[rows, ln].astype(BF16) for rows, ln in ent], [g_ref[rows, ln] for rows, ln in ent],
            sts, [lb[:, ln] for _, ln in ent], [oml[:, ln] for _, ln in ent],
            ng, tri_ref[...], lvl_ref[...], row)
        for (rows, ln), yy in zip(ent, y):
            o_ref[rows, ln] = yy.astype(o_ref.dtype)
        return tuple(new)

    lax.fori_loop(0, seq // (c * nch), body, tuple(jnp.zeros((HG_DIM, HG_DIM), F32) for _ in range(heads)))


def _hgrn2(proj, lb_param, norm_g, batch):
    c, hp = HG_CHUNK, HG_HEADS_PER_STEP
    tri, lvl = _hgrn_tables(c)
    m = proj.shape[0]
    w = hp * LANES

    def col(off):
        return pl.BlockSpec((SEQ, w), lambda b, h, o=off // hp: (b, o + h))

    return pl.pallas_call(
        functools.partial(_hgrn_kernel, chunk=c, heads=hp),
        grid=(batch, HG_HEADS // hp),
        in_specs=[
            col(_OFF_HQ), col(_OFF_HF), col(_OFF_HI), col(_OFF_HG),
            pl.BlockSpec((lb_param.shape[0], w), lambda b, h: (0, h)),
            pl.BlockSpec((1, LANES), lambda b, h: (0, 0)),
            pl.BlockSpec((c, c), lambda b, h: (0, 0)),
            pl.BlockSpec((c, c), lambda b, h: (0, 0)),
        ],
        out_specs=pl.BlockSpec((SEQ, w), lambda b, h: (b, h)),
        out_shape=jax.ShapeDtypeStruct((m, HG_WIDTH), BF16),
        compiler_params=_params("parallel", "parallel"),
        name="hgrn2",
    )(proj, proj, proj, proj, lb_param, norm_g.reshape(1, LANES), tri, lvl)


ATTN_BLOCK = 128
ATTN_BATCH = 8


def _attn_kernel(q0, k0, v0, q1, k1, v1, q2, k2, v2, o_ref, og_ref, lse_ref):
    blk = ATTN_BLOCK
    scale = HEAD_DIM ** -0.5
    qi = lax.broadcasted_iota(jnp.int32, (blk, 2 * blk), 0)
    ki = lax.broadcasted_iota(jnp.int32, (blk, 2 * blk), 1)
    mask2 = (ki >= qi) & (ki <= qi + blk)
    mask1 = (lax.broadcasted_iota(jnp.int32, (blk, blk), 1)
             <= lax.broadcasted_iota(jnp.int32, (blk, blk), 0))

    def run_batch(items):
        s = [_dot_nt(qj, kw) * (scale * LOG2E) for qj, kw, _, _, _ in items]
        ps, mxs = [], []
        for (_, _, _, mask, _), sj in zip(items, s):
            sj = jnp.where(mask, sj, NEG_INF)
            mx = jnp.max(sj, axis=-1, keepdims=True)
            ps.append(jnp.exp2(sj - mx).astype(BF16))
            mxs.append(mx)
        os = [_dot(p, jnp.concatenate([vw, jnp.ones_like(vw)], axis=1)) for p, (_, _, vw, _, _) in zip(ps, items)]
        for (_, _, _, _, (g, rows)), o, mx in zip(items, os, mxs):
            den = o[:, HEAD_DIM:]
            og_ref.at[g][rows, :] = o[:, :HEAD_DIM] * (1.0 / den)
            lse_ref.at[g][rows, :] = mx + jnp.log(den) * LOG2E

    refs = ((q0, k0, v0), (q1, k1, v1), (q2, k2, v2))
    pending = []
    for g, (window, dil) in enumerate(ATTN_GROUPS):
        assert window // dil == blk
        q_ref, k_ref, v_ref = refs[g]
        length = SEQ // dil
        for r in range(dil):
            base = (_attn_slot(r) if dil == 16 else r) * length
            for j in range(length // blk):
                lo, hi = base + max(j - 1, 0) * blk, base + (j + 1) * blk
                rows = pl.ds(j * blk, blk) if dil == 1 else pl.ds(j * blk * dil + r, blk, stride=dil)
                pending.append((q_ref[hi - blk:hi, :], k_ref[lo:hi, :], v_ref[lo:hi, :],
                                mask1 if j == 0 else mask2, (g, rows)))
                if len(pending) == ATTN_BATCH:
                    run_batch(pending)
                    pending = []
    assert not pending

    l0, l1, l2 = lse_ref[0], lse_ref[1], lse_ref[2]
    mx = jnp.maximum(jnp.maximum(l0, l1), l2)
    w0, w1, w2 = jnp.exp2(l0 - mx), jnp.exp2(l1 - mx), jnp.exp2(l2 - mx)
    o = (w0 * og_ref[0] + w1 * og_ref[1] + w2 * og_ref[2]) / (w0 + w1 + w2)
    o_ref[...] = o.astype(o_ref.dtype)


def _attention(qkv, batch):
    m = qkv[0].shape[0]
    hg = ATTN_HEADS_PER_GROUP

    def col(part):
        return pl.BlockSpec((SEQ, LANES), lambda b, h, o=part * hg: (b, o + h))

    return pl.pallas_call(
        _attn_kernel,
        grid=(batch, hg),
        in_specs=[col(0), col(1), col(2)] * len(ATTN_GROUPS),
        out_specs=pl.BlockSpec((SEQ, LANES), lambda b, h: (b, h)),
        out_shape=jax.ShapeDtypeStruct((m, ATTN_OUT), BF16),
        scratch_shapes=[pltpu.VMEM((len(ATTN_GROUPS), SEQ, LANES), F32)] * 2,
        compiler_params=_params("parallel", "parallel"),
        name="dilated_attn",
    )(*[a for g in qkv for a in (g, g, g)])


MIX_GATE_COLS = 1024


def _mix_kernel(*refs):
    nc = D_MODEL // MIX_GATE_COLS
    oa_ref, ob_ref = refs[0], refs[1]
    ga_refs, gb_refs = refs[2:2 + nc], refs[2 + nc:2 + 2 * nc]
    x_ref, wpa_ref, wpb_ref, wo_ref, n2_ref, h_ref, v_ref = refs[2 + 2 * nc:]
    oa, ob = oa_ref[...], ob_ref[...]
    parts = []
    for c in range(nc):
        cols = slice(c * MIX_GATE_COLS, (c + 1) * MIX_GATE_COLS)
        ya = _dot(oa, wpa_ref[:, cols])
        yb = _dot(ob, wpb_ref[:, cols])
        mix = jax.nn.sigmoid(ga_refs[c][...]) * ya + jax.nn.sigmoid(gb_refs[c][...]) * yb
        parts.append(mix.astype(BF16))
    h = x_ref[...] + _dot(jnp.concatenate(parts, axis=1), wo_ref[...])
    h_ref[...] = h
    v_ref[...] = _rmsnorm_rows(h, n2_ref[...]).astype(v_ref.dtype)


def _mix(oa, ob, proj, x2, wpa, wpb, wo, norm2_g, tm=256):
    m, d = x2.shape
    gc = MIX_GATE_COLS
    nc = d // gc
    resident = dict(pipeline_mode=pl.Buffered(1))

    def gate(off, c):
        return pl.BlockSpec((tm, gc), lambda i, o=off * LANES // gc + c: (i, o))

    return pl.pallas_call(
        _mix_kernel,
        grid=(m // tm,),
        in_specs=[
            pl.BlockSpec((tm, HG_WIDTH), lambda i: (i, 0)),
            pl.BlockSpec((tm, ATTN_OUT), lambda i: (i, 0)),
            *[gate(_OFF_GA, c) for c in range(nc)],
            *[gate(_OFF_GB, c) for c in range(nc)],
            pl.BlockSpec((tm, d), lambda i: (i, 0)),
            pl.BlockSpec(wpa.shape, lambda i: (0, 0), **resident),
            pl.BlockSpec(wpb.shape, lambda i: (0, 0), **resident),
            pl.BlockSpec(wo.shape, lambda i: (0, 0), **resident),
            pl.BlockSpec((1, d), lambda i: (0, 0)),
        ],
        out_specs=[pl.BlockSpec((tm, d), lambda i: (i, 0)), pl.BlockSpec((tm, d), lambda i: (i, 0))],
        out_shape=[jax.ShapeDtypeStruct((m, d), F32), jax.ShapeDtypeStruct((m, d), BF16)],
        compiler_params=_params("parallel"),
        name="mix_outproj",
    )(oa, ob, *([proj] * (2 * nc)), x2, wpa, wpb, wo, norm2_g.reshape(1, d))


def _shift_rows(h, prev, k):
    r = pltpu.roll(h, k, axis=0)
    p = pltpu.roll(prev, k, axis=0)
    row8 = lax.broadcasted_iota(jnp.int32, prev.shape, 0)
    top = jnp.where(row8 < k, p, r[:8])
    return jnp.concatenate([top, r[8:]], axis=0)


def _causal_conv3(h, prev, w, b):
    return w[2:3] * h + w[1:2] * _shift_rows(h, prev, 1) + w[0:1] * _shift_rows(h, prev, 2) + b


def _gelu_tanh_gate(x, y):
    c = math.sqrt(2.0 / math.pi)
    w = x * ((-2.0 * c * LOG2E) + (-2.0 * c * 0.044715 * LOG2E) * (x * x))
    return (x * y) * (1.0 / (1.0 + jnp.exp2(w)))


def _ffn_up_kernel(v_ref, halo_ref, wa_ref, wb_ref, cwa_ref, cwb_ref, cba_ref, cbb_ref, o_ref, *, sub):
    tm = v_ref.shape[0]
    hr = halo_ref.shape[0]
    starts_sequence = (pl.program_id(0) * tm) % SEQ == 0
    halo = jnp.where(starts_sequence, jnp.zeros_like(halo_ref[...]), halo_ref[...])
    wa, wb = wa_ref[...].astype(BF16), wb_ref[...].astype(BF16)
    cwa, cwb, cba, cbb = cwa_ref[...], cwb_ref[...], cba_ref[...], cbb_ref[...]
    pa = pb = None
    for r0 in range(0, tm, sub):
        vs = v_ref[r0:r0 + sub, :]
        if r0 == 0:
            vs = jnp.concatenate([halo, vs], axis=0)
        ha, hb = _dot(vs, wa), _dot(vs, wb)
        if r0 == 0:
            pa, pb = ha[hr - 8:hr], hb[hr - 8:hr]
            ha, hb = ha[hr:], hb[hr:]
        ca = _causal_conv3(ha, pa, cwa, cba)
        cb = _causal_conv3(hb, pb, cwb, cbb)
        o_ref[r0:r0 + sub, :] = _gelu_tanh_gate(ca, cb).astype(o_ref.dtype)
        pa, pb = ha[sub - 8:], hb[sub - 8:]


def _ffn_up(v, w_up, conv_w, conv_b, tm=2048, tn=512, sub=512):
    m, d = v.shape
    nj = D_FF // tn
    hb = tm // BF16_SUBLANES
    return pl.pallas_call(
        functools.partial(_ffn_up_kernel, sub=sub),
        grid=(m // tm, nj),
        in_specs=[
            pl.BlockSpec((tm, d), lambda i, j: (i, 0)),
            pl.BlockSpec((BF16_SUBLANES, d), lambda i, j: (jnp.maximum(i * hb - 1, 0), 0)),
            pl.BlockSpec((d, tn), lambda i, j: (0, j)),
            pl.BlockSpec((d, tn), lambda i, j: (0, j + nj)),
            pl.BlockSpec((3, tn), lambda i, j: (0, j)),
            pl.BlockSpec((3, tn), lambda i, j: (0, j + nj)),
            pl.BlockSpec((1, tn), lambda i, j: (0, j)),
            pl.BlockSpec((1, tn), lambda i, j: (0, j + nj)),
        ],
        out_specs=pl.BlockSpec((tm, tn), lambda i, j: (i, j)),
        out_shape=jax.ShapeDtypeStruct((m, D_FF), BF16),
        compiler_params=_params("parallel", "parallel"),
        name="ffn_up_conv_gate",
    )(v, v, w_up, w_up, conv_w, conv_w, conv_b.reshape(1, -1), conv_b.reshape(1, -1))


def _ffn_down_kernel(a_ref, w_ref, h_ref, g_ref, o_ref):
    o_ref[...] = _rmsnorm_rows(h_ref[...] + _dot(a_ref[...], w_ref[...]), g_ref[...])


def _ffn_down(act, w_down, h1, norm_g, tm=512):
    m, kdim = act.shape
    d = w_down.shape[1]
    return pl.pallas_call(
        _ffn_down_kernel,
        grid=(m // tm,),
        in_specs=[
            pl.BlockSpec((tm, kdim), lambda i: (i, 0)),
            pl.BlockSpec((kdim, d), lambda i: (0, 0), pipeline_mode=pl.Buffered(1)),
            pl.BlockSpec((tm, d), lambda i: (i, 0)),
            pl.BlockSpec((1, d), lambda i: (0, 0)),
        ],
        out_specs=pl.BlockSpec((tm, d), lambda i: (i, 0)),
        out_shape=jax.ShapeDtypeStruct((m, d), F32),
        compiler_params=_params("parallel"),
        name="ffn_down_norm",
    )(act, w_down, h1, norm_g.reshape(1, d))


def kernel(x, norm1_g, w_in, hg_lb_param, hg_norm_g, w_proj_a, w_proj_b, w_out,
           norm2_g, w_up, conv_w, conv_b, w_down, norm_f_g):
    batch, seq, d = x.shape
    depth = w_in.shape[0]
    assert (seq, d, depth) == (SEQ, D_MODEL, 1) and w_in.shape[2] == IN_COLS
    h = x.reshape(batch * seq, d)
    for l in range(depth):
        u = _norm_bf16(h, norm1_g[l])
        proj, (wpa, wpb, wo, wdn) = _in_proj(u, w_in[l], (w_proj_a[l], w_proj_b[l], w_out[l], w_down[l]))
        oa = _hgrn2(proj, hg_lb_param, hg_norm_g[l], batch)
        ob = _attention([_attn_proj(u, w_in[l], g) for g in range(len(ATTN_GROUPS))], batch)
        h, v = _mix(oa, ob, proj, h, wpa, wpb, wo, norm2_g[l])
        act = _ffn_up(v, w_up[l], conv_w[l], conv_b[l])
        out = _ffn_down(act, wdn, h, norm_f_g)
    return out.reshape(batch, seq, d)
```

```python
import functools
import math

import numpy as np
import jax
import jax.numpy as jnp
from jax import lax
from jax.experimental import pallas as pl
from jax.experimental.pallas import tpu as pltpu

D_MODEL = 2048
SEQ = 2048
HG_HEADS = 16
HG_DIM = 128
HG_WIDTH = HG_HEADS * HG_DIM
ATTN_GROUPS = ((128, 1), (512, 4), (2048, 16))
ATTN_HEADS_PER_GROUP = 4
HEAD_DIM = 128
ATTN_WIDTH = len(ATTN_GROUPS) * ATTN_HEADS_PER_GROUP * HEAD_DIM
ATTN_OUT = ATTN_HEADS_PER_GROUP * HEAD_DIM
ROPE_THETA = 10000.0
D_FF = 5632
NORM_EPS = 1e-6
NEG_INF = -1e30
IN_COLS = 4 * HG_WIDTH + 3 * ATTN_WIDTH + 2 * D_MODEL

_OFF_HQ = 0
_OFF_HF = HG_WIDTH // 128
_OFF_HI = 2 * HG_WIDTH // 128
_OFF_HG = 3 * HG_WIDTH // 128
_OFF_AQ = 4 * HG_WIDTH // 128
_OFF_AK = _OFF_AQ + ATTN_WIDTH // 128
_OFF_AV = _OFF_AK + ATTN_WIDTH // 128
_OFF_GA = 4 * HG_WIDTH // 128
_OFF_GB = _OFF_GA + D_MODEL // 128

LANES = 128
BF16_SUBLANES = 16
VMEM_LIMIT = 56 * 1024 * 1024
LOG2E = 1.4426950408889634

F32 = jnp.float32
BF16 = jnp.bfloat16


def _dot(a, b):
    return jnp.dot(a, b, preferred_element_type=F32)


def _dot_nt(a, b):
    return lax.dot_general(a, b, (((1,), (1,)), ((), ())), preferred_element_type=F32)


def _dot_tn(a, b):
    return lax.dot_general(a, b, (((0,), (0,)), ((), ())), preferred_element_type=F32)


def _rmsnorm_rows(x, g):
    ms = jnp.mean(x * x, axis=-1, keepdims=True)
    return x * lax.rsqrt(ms + NORM_EPS) * g


def _sigmoid(x):
    return 1.0 / (1.0 + jnp.exp2(x * (-LOG2E)))


def _params(*sem):
    return pltpu.CompilerParams(dimension_semantics=sem, vmem_limit_bytes=VMEM_LIMIT)


def _norm_kernel(x_ref, g_ref, u_ref):
    u_ref[...] = _rmsnorm_rows(x_ref[...], g_ref[...]).astype(u_ref.dtype)


def _norm_bf16(x2, norm_g, tm=1024):
    m, d = x2.shape
    return pl.pallas_call(
        _norm_kernel,
        grid=(m // tm,),
        in_specs=[pl.BlockSpec((tm, d), lambda i: (i, 0)), pl.BlockSpec((1, d), lambda i: (0, 0))],
        out_specs=pl.BlockSpec((tm, d), lambda i: (i, 0)),
        out_shape=jax.ShapeDtypeStruct((m, d), BF16),
        compiler_params=_params("parallel"),
        name="norm1",
    )(x2, norm_g.reshape(1, d))


def _inproj_kernel(u_ref, w_ref, *refs):
    n_side = (len(refs) - 1) // 2
    o_ref = refs[n_side]
    o_ref[...] = _dot(u_ref[...], w_ref[...].astype(BF16))
    for src, dst in zip(refs[:n_side], refs[n_side + 1:]):
        dst[...] = src[...].astype(dst.dtype)


def _in_proj(u, w, side_weights, tm=4096, tn=512):
    m, d = u.shape
    n_hg, skip = 4 * HG_WIDTH // tn, 3 * ATTN_WIDTH // tn
    nj = n_hg + 2 * D_MODEL // tn
    steps = (m // tm) * nj
    side_in, side_out, side_shape = [], [], []
    for sw in side_weights:
        rows = -(-sw.shape[0] // steps)
        rows = -(-rows // BF16_SUBLANES) * BF16_SUBLANES
        while sw.shape[0] % rows:
            rows += BF16_SUBLANES
        last = sw.shape[0] // rows - 1
        spec = pl.BlockSpec((rows, sw.shape[1]), lambda i, j, last=last: (jnp.minimum(i * nj + j, last), 0))
        side_in.append(spec)
        side_out.append(spec)
        side_shape.append(jax.ShapeDtypeStruct(sw.shape, BF16))
    outs = pl.pallas_call(
        _inproj_kernel,
        grid=(m // tm, nj),
        in_specs=[
            pl.BlockSpec((tm, d), lambda i, j: (i, 0), pipeline_mode=pl.Buffered(1)),
            pl.BlockSpec((d, tn), lambda i, j: (0, jnp.where(j < n_hg, j, j + skip))),
            *side_in,
        ],
        out_specs=[pl.BlockSpec((tm, tn), lambda i, j: (i, j)), *side_out],
        out_shape=[jax.ShapeDtypeStruct((m, nj * tn), F32), *side_shape],
        compiler_params=_params("arbitrary", "arbitrary"),
        name="in_proj",
    )(u, w, *side_weights)
    return outs[0], outs[1:]


def _attn_proj_kernel(u_ref, w_ref, cos_ref, sin_ref, o_ref, scr, scr2, *, dil, sub):
    w = w_ref[...].astype(BF16)
    tm = u_ref.shape[0]
    length, n = tm // dil, sub // dil
    for r0 in range(0, tm, sub):
        acc = _dot(u_ref[r0:r0 + sub, :], w)
        cs, sn = cos_ref[r0:r0 + sub, :], sin_ref[r0:r0 + sub, :]
        for hh in range(acc.shape[1] // HEAD_DIM):
            ln = slice(hh * HEAD_DIM, (hh + 1) * HEAD_DIM)
            x = acc[:, ln]
            y = x * cs + pltpu.roll(x, HEAD_DIM // 2, axis=1) * sn
            if dil == 1:
                o_ref[r0:r0 + sub, ln] = y.astype(o_ref.dtype)
                continue
            scr[hh] = y
            for r1 in range(4):
                part = scr[hh, pl.ds(r1, sub // 4, stride=4), :]
                if dil == 4:
                    dst = r1 * length + r0 // dil
                    o_ref[dst:dst + n, ln] = part.astype(o_ref.dtype)
                    continue
                scr2[hh, r1] = part
                for r2 in range(4):
                    dst = _attn_slot(r1 + 4 * r2) * length + r0 // dil
                    o_ref[dst:dst + n, ln] = scr2[hh, r1, pl.ds(r2, n, stride=4), :].astype(o_ref.dtype)


def _attn_slot(r):
    return (r % 4) * 4 + r // 4


def _attn_proj(u, w, group, tm=SEQ, sub=512):
    m, d = u.shape
    dil = ATTN_GROUPS[group][1]
    assert dil in (1, 4, 16)
    tn = ATTN_HEADS_PER_GROUP * HEAD_DIM
    half = HEAD_DIM // 2
    inv_freq = jnp.exp(-math.log(ROPE_THETA) * jnp.arange(half, dtype=F32) * (2.0 / HEAD_DIM))
    ang = jnp.arange(SEQ, dtype=F32)[:, None] * inv_freq[None, :]
    cos, sin = jnp.cos(ang), jnp.sin(ang)
    one, zero = jnp.ones((SEQ, HEAD_DIM), F32), jnp.zeros((SEQ, HEAD_DIM), F32)
    cos_t = jnp.stack([jnp.concatenate([cos, cos], axis=1), one])
    sin_t = jnp.stack([jnp.concatenate([-sin, sin], axis=1), zero])
    first = (_OFF_AQ * LANES) // tn + group
    per = ATTN_WIDTH // tn
    table = pl.BlockSpec((None, SEQ, HEAD_DIM), lambda i, j: (j // 2, 0, 0))
    return pl.pallas_call(
        functools.partial(_attn_proj_kernel, dil=dil, sub=sub),
        grid=(m // tm, 3),
        in_specs=[
            pl.BlockSpec((tm, d), lambda i, j: (i, 0)),
            pl.BlockSpec((d, tn), lambda i, j: (0, first + per * j)),
            table, table,
        ],
        out_specs=pl.BlockSpec((tm, tn), lambda i, j: (i, j)),
        out_shape=jax.ShapeDtypeStruct((m, 3 * tn), BF16),
        scratch_shapes=[pltpu.VMEM((tn // HEAD_DIM, sub, HEAD_DIM), F32),
                        pltpu.VMEM((tn // HEAD_DIM, 4, sub // 4, HEAD_DIM), F32)],
        compiler_params=_params("parallel", "parallel"),
        name="attn_proj",
    )(u, w, cos_t, sin_t)


HG_CHUNK = 128
HG_HEADS_PER_STEP = 2
HG_CHUNKS_PER_STEP = 8


def _hgrn_tables(c):
    t = np.arange(c)[:, None]
    s = np.arange(c)[None, :]
    x = t ^ s
    lvl = np.where(x > 0, np.floor(np.log2(np.maximum(x, 1))), -1).astype(np.int32)
    lvl = np.where(s > t, -2, lvl).astype(np.int32)
    tri = (s <= t).astype(np.float32)
    return jnp.asarray(tri, BF16), jnp.asarray(lvl, jnp.int32)


def _neg_abs(d):
    bits = lax.bitcast_convert_type(d, jnp.uint32) | jnp.uint32(0x80000000)
    return lax.bitcast_convert_type(bits, F32)


def _hgrn_chunk(zq, zf, vv, zg, sts, lb, oml, ng, tri, lvl, row):
    heads = range(len(zq))
    c = zq[0].shape[0]
    f = [lb[i] + oml[i] * _sigmoid(zf[i]) for i in heads]
    kk = [1.0 - f[i] for i in heads]
    qq = [zq[i] * _sigmoid(zq[i]) for i in heads]
    gl = [jnp.log(f[i]) * LOG2E for i in heads]

    gcum = []
    for i in heads:
        g1 = gl[i].astype(BF16)
        g2 = (gl[i] - g1.astype(F32)).astype(BF16)
        gc = _dot(tri, jnp.concatenate([g1, g2], axis=1))
        gcum.append(gc[:, :LANES] + gc[:, LANES:])
    glast = [gcum[i][c - 1:c, :] for i in heads]

    qb = [qq[i].astype(BF16) for i in heads]
    kb = [kk[i].astype(BF16) for i in heads]
    st_in, a = list(sts), []
    for i in heads:
        kd = (kk[i] * jnp.exp2(glast[i] - gcum[i])).astype(BF16)
        st_in.append(st_in[i] * jnp.exp2(glast[i]) + _dot_tn(vv[i], kd))
        a.append(jnp.where(lvl == -1, _dot_nt(qb[i], kb[i]), 0.0))

    h = c // 2
    while h >= 1:
        in_level = lvl == int(math.log2(h))
        for i in heads:
            if h >= 4:
                g3d = gcum[i].reshape(c // (2 * h), 2 * h, LANES)
                ref = jnp.broadcast_to(g3d[:, h - 1:h, :], g3d.shape).reshape(c, LANES)
                x = _neg_abs(gcum[i] - ref)
            elif h == 2:
                pos = row & 3
                up = pltpu.roll(gl[i], c - 1, axis=0)
                dn = pltpu.roll(gl[i], 1, axis=0)
                x = jnp.where(pos == 0, up, jnp.where(pos == 1, 0.0, jnp.where(pos == 2, gl[i], gl[i] + dn)))
            else:
                x = jnp.where((row & 1) == 1, gl[i], 0.0)
            e = jnp.exp2(x).astype(BF16)
            a[i] = jnp.where(in_level, _dot_nt(qb[i] * e, kb[i] * e), a[i])
        h //= 2
    o = []
    for i in heads:
        qg = (qq[i] * jnp.exp2(gcum[i])).astype(BF16)
        o.append(_dot_nt(qg, st_in[i].astype(BF16)) + _dot(a[i].astype(BF16), vv[i]))
    y = [_rmsnorm_rows(o[i], ng) * (zg[i] * _sigmoid(zg[i])) for i in heads]
    return y, st_in[len(zq):]


def _hgrn_kernel(q_ref, f_ref, i_ref, g_ref, lbp_ref, ng_ref, tri_ref, lvl_ref, o_ref, *, chunk, heads):
    c = chunk
    seq = q_ref.shape[0]
    p = lbp_ref[...]
    pe = jnp.exp(p - jnp.max(p, axis=0, keepdims=True))
    lb = pe[0:1, :] / jnp.sum(pe, axis=0, keepdims=True)
    oml = 1.0 - lb
    ng = ng_ref[...]
    row = lax.broadcasted_iota(jnp.int32, (c, LANES), 0)
    lanes = [slice(hh * LANES, (hh + 1) * LANES) for hh in range(heads)]
    nch = HG_CHUNKS_PER_STEP

    def body(ci, sts):
        ent = [(pl.ds(pl.multiple_of((ci * nch + ch) * c, c), c), ln) for ch in range(nch) for ln in lanes]
        y, new = _hgrn_chunk(
            [q_ref[rows, ln] for rows, ln in ent], [f_ref[rows, ln] for rows, ln in ent],
            [i_ref[rows, ln].astype(BF16) for rows, ln in ent], [g_ref[rows, ln] for rows, ln in ent],
            sts, [lb[:, ln] for _, ln in ent], [oml[:, ln] for _, ln in ent],
            ng, tri_ref[...], lvl_ref[...], row)
        for (rows, ln), yy in zip(ent, y):
            o_ref[rows, ln] = yy.astype(o_ref.dtype)
        return tuple(new)

    lax.fori_loop(0, seq // (c * nch), body, tuple(jnp.zeros((HG_DIM, HG_DIM), F32) for _ in range(heads)))


def _hgrn2(proj, lb_param, norm_g, batch):
    c, hp = HG_CHUNK, HG_HEADS_PER_STEP
    tri, lvl = _hgrn_tables(c)
    m = proj.shape[0]
    w = hp * LANES

    def col(off):
        return pl.BlockSpec((SEQ, w), lambda b, h, o=off // hp: (b, o + h))

    return pl.pallas_call(
        functools.partial(_hgrn_kernel, chunk=c, heads=hp),
        grid=(batch, HG_HEADS // hp),
        in_specs=[
            col(_OFF_HQ), col(_OFF_HF), col(_OFF_HI), col(_OFF_HG),
            pl.BlockSpec((lb_param.shape[0], w), lambda b, h: (0, h)),
            pl.BlockSpec((1, LANES), lambda b, h: (0, 0)),
            pl.BlockSpec((c, c), lambda b, h: (0, 0)),
            pl.BlockSpec((c, c), lambda b, h: (0, 0)),
        ],
        out_specs=pl.BlockSpec((SEQ, w), lambda b, h: (b, h)),
        out_shape=jax.ShapeDtypeStruct((m, HG_WIDTH), BF16),
        compiler_params=_params("parallel", "parallel"),
        name="hgrn2",
    )(proj, proj, proj, proj, lb_param, norm_g.reshape(1, LANES), tri, lvl)


ATTN_BLOCK = 128
ATTN_BATCH = 8


def _attn_kernel(q0, k0, v0, q1, k1, v1, q2, k2, v2, o_ref, og_ref, lse_ref):
    blk = ATTN_BLOCK
    scale = HEAD_DIM ** -0.5
    qi = lax.broadcasted_iota(jnp.int32, (blk, 2 * blk), 0)
    ki = lax.broadcasted_iota(jnp.int32, (blk, 2 * blk), 1)
    mask2 = (ki >= qi) & (ki <= qi + blk)
    mask1 = (lax.broadcasted_iota(jnp.int32, (blk, blk), 1)
             <= lax.broadcasted_iota(jnp.int32, (blk, blk), 0))

    def run_batch(items):
        s = [_dot_nt(qj, kw) * (scale * LOG2E) for qj, kw, _, _, _ in items]
        ps, mxs = [], []
        for (_, _, _, mask, _), sj in zip(items, s):
            sj = jnp.where(mask, sj, NEG_INF)
            mx = jnp.max(sj, axis=-1, keepdims=True)
            ps.append(jnp.exp2(sj - mx).astype(BF16))
            mxs.append(mx)
        os = [_dot(p, jnp.concatenate([vw, jnp.ones_like(vw)], axis=1)) for p, (_, _, vw, _, _) in zip(ps, items)]
        for (_, _, _, _, (g, rows)), o, mx in zip(items, os, mxs):
            den = o[:, HEAD_DIM:]
            og_ref.at[g][rows, :] = o[:, :HEAD_DIM] * (1.0 / den)
            lse_ref.at[g][rows, :] = mx + jnp.log(den) * LOG2E

    refs = ((q0, k0, v0), (q1, k1, v1), (q2, k2, v2))
    pending = []
    for g, (window, dil) in enumerate(ATTN_GROUPS):
        assert window // dil == blk
        q_ref, k_ref, v_ref = refs[g]
        length = SEQ // dil
        for r in range(dil):
            base = (_attn_slot(r) if dil == 16 else r) * length
            for j in range(length // blk):
                lo, hi = base + max(j - 1, 0) * blk, base + (j + 1) * blk
                rows = pl.ds(j * blk, blk) if dil == 1 else pl.ds(j * blk * dil + r, blk, stride=dil)
                pending.append((q_ref[hi - blk:hi, :], k_ref[lo:hi, :], v_ref[lo:hi, :],
                                mask1 if j == 0 else mask2, (g, rows)))
                if len(pending) == ATTN_BATCH:
                    run_batch(pending)
                    pending = []
    assert not pending

    l0, l1, l2 = lse_ref[0], lse_ref[1], lse_ref[2]
    mx = jnp.maximum(jnp.maximum(l0, l1), l2)
    w0, w1, w2 = jnp.exp2(l0 - mx), jnp.exp2(l1 - mx), jnp.exp2(l2 - mx)
    o = (w0 * og_ref[0] + w1 * og_ref[1] + w2 * og_ref[2]) / (w0 + w1 + w2)
    o_ref[...] = o.astype(o_ref.dtype)


def _attention(qkv, batch):
    m = qkv[0].shape[0]
    hg = ATTN_HEADS_PER_GROUP

    def col(part):
        return pl.BlockSpec((SEQ, LANES), lambda b, h, o=part * hg: (b, o + h))

    return pl.pallas_call(
        _attn_kernel,
        grid=(batch, hg),
        in_specs=[col(0), col(1), col(2)] * len(ATTN_GROUPS),
        out_specs=pl.BlockSpec((SEQ, LANES), lambda b, h: (b, h)),
        out_shape=jax.ShapeDtypeStruct((m, ATTN_OUT), BF16),
        scratch_shapes=[pltpu.VMEM((len(ATTN_GROUPS), SEQ, LANES), F32)] * 2,
        compiler_params=_params("parallel", "parallel"),
        name="dilated_attn",
    )(*[a for g in qkv for a in (g, g, g)])


MIX_GATE_COLS = 1024


def _mix_kernel(*refs):
    nc = D_MODEL // MIX_GATE_COLS
    oa_ref, ob_ref = refs[0], refs[1]
    ga_refs, gb_refs = refs[2:2 + nc], refs[2 + nc:2 + 2 * nc]
    x_ref, wpa_ref, wpb_ref, wo_ref, n2_ref, h_ref, v_ref = refs[2 + 2 * nc:]
    oa, ob = oa_ref[...], ob_ref[...]
    parts = []
    for c in range(nc):
        cols = slice(c * MIX_GATE_COLS, (c + 1) * MIX_GATE_COLS)
        ya = _dot(oa, wpa_ref[:, cols])
        yb = _dot(ob, wpb_ref[:, cols])
        mix = jax.nn.sigmoid(ga_refs[c][...]) * ya + jax.nn.sigmoid(gb_refs[c][...]) * yb
        parts.append(mix.astype(BF16))
    h = x_ref[...] + _dot(jnp.concatenate(parts, axis=1), wo_ref[...])
    h_ref[...] = h
    v_ref[...] = _rmsnorm_rows(h, n2_ref[...]).astype(v_ref.dtype)


def _mix(oa, ob, proj, x2, wpa, wpb, wo, norm2_g, tm=256):
    m, d = x2.shape
    gc = MIX_GATE_COLS
    nc = d // gc
    resident = dict(pipeline_mode=pl.Buffered(1))

    def gate(off, c):
        return pl.BlockSpec((tm, gc), lambda i, o=off * LANES // gc + c: (i, o))

    return pl.pallas_call(
        _mix_kernel,
        grid=(m // tm,),
        in_specs=[
            pl.BlockSpec((tm, HG_WIDTH), lambda i: (i, 0)),
            pl.BlockSpec((tm, ATTN_OUT), lambda i: (i, 0)),
            *[gate(_OFF_GA, c) for c in range(nc)],
            *[gate(_OFF_GB, c) for c in range(nc)],
            pl.BlockSpec((tm, d), lambda i: (i, 0)),
            pl.BlockSpec(wpa.shape, lambda i: (0, 0), **resident),
            pl.BlockSpec(wpb.shape, lambda i: (0, 0), **resident),
            pl.BlockSpec(wo.shape, lambda i: (0, 0), **resident),
            pl.BlockSpec((1, d), lambda i: (0, 0)),
        ],
        out_specs=[pl.BlockSpec((tm, d), lambda i: (i, 0)), pl.BlockSpec((tm, d), lambda i: (i, 0))],
        out_shape=[jax.ShapeDtypeStruct((m, d), F32), jax.ShapeDtypeStruct((m, d), BF16)],
        compiler_params=_params("parallel"),
        name="mix_outproj",
    )(oa, ob, *([proj] * (2 * nc)), x2, wpa, wpb, wo, norm2_g.reshape(1, d))


def _shift_rows(h, prev, k):
    r = pltpu.roll(h, k, axis=0)
    p = pltpu.roll(prev, k, axis=0)
    row8 = lax.broadcasted_iota(jnp.int32, prev.shape, 0)
    top = jnp.where(row8 < k, p, r[:8])
    return jnp.concatenate([top, r[8:]], axis=0)


def _causal_conv3(h, prev, w, b):
    return w[2:3] * h + w[1:2] * _shift_rows(h, prev, 1) + w[0:1] * _shift_rows(h, prev, 2) + b


def _gelu_tanh_gate(x, y):
    c = math.sqrt(2.0 / math.pi)
    w = x * ((-2.0 * c * LOG2E) + (-2.0 * c * 0.044715 * LOG2E) * (x * x))
    return (x * y) * (1.0 / (1.0 + jnp.exp2(w)))


def _ffn_up_kernel(v_ref, halo_ref, wa_ref, wb_ref, cwa_ref, cwb_ref, cba_ref, cbb_ref, o_ref, *, sub):
    tm = v_ref.shape[0]
    hr = halo_ref.shape[0]
    starts_sequence = (pl.program_id(0) * tm) % SEQ == 0
    halo = jnp.where(starts_sequence, jnp.zeros_like(halo_ref[...]), halo_ref[...])
    wa, wb = wa_ref[...].astype(BF16), wb_ref[...].astype(BF16)
    cwa, cwb, cba, cbb = cwa_ref[...], cwb_ref[...], cba_ref[...], cbb_ref[...]
    pa = pb = None
    for r0 in range(0, tm, sub):
        vs = v_ref[r0:r0 + sub, :]
        if r0 == 0:
            vs = jnp.concatenate([halo, vs], axis=0)
        ha, hb = _dot(vs, wa), _dot(vs, wb)
        if r0 == 0:
            pa, pb = ha[hr - 8:hr], hb[hr - 8:hr]
            ha, hb = ha[hr:], hb[hr:]
        ca = _causal_conv3(ha, pa, cwa, cba)
        cb = _causal_conv3(hb, pb, cwb, cbb)
        o_ref[r0:r0 + sub, :] = _gelu_tanh_gate(ca, cb).astype(o_ref.dtype)
        pa, pb = ha[sub - 8:], hb[sub - 8:]


def _ffn_up(v, w_up, conv_w, conv_b, tm=2048, tn=512, sub=512):
    m, d = v.shape
    nj = D_FF // tn
    hb = tm // BF16_SUBLANES
    return pl.pallas_call(
        functools.partial(_ffn_up_kernel, sub=sub),
        grid=(m // tm, nj),
        in_specs=[
            pl.BlockSpec((tm, d), lambda i, j: (i, 0)),
            pl.BlockSpec((BF16_SUBLANES, d), lambda i, j: (jnp.maximum(i * hb - 1, 0), 0)),
            pl.BlockSpec((d, tn), lambda i, j: (0, j)),
            pl.BlockSpec((d, tn), lambda i, j: (0, j + nj)),
            pl.BlockSpec((3, tn), lambda i, j: (0, j)),
            pl.BlockSpec((3, tn), lambda i, j: (0, j + nj)),
            pl.BlockSpec((1, tn), lambda i, j: (0, j)),
            pl.BlockSpec((1, tn), lambda i, j: (0, j + nj)),
        ],
        out_specs=pl.BlockSpec((tm, tn), lambda i, j: (i, j)),
        out_shape=jax.ShapeDtypeStruct((m, D_FF), BF16),
        compiler_params=_params("parallel", "parallel"),
        name="ffn_up_conv_gate",
    )(v, v, w_up, w_up, conv_w, conv_w, conv_b.reshape(1, -1), conv_b.reshape(1, -1))


def _ffn_down_kernel(a_ref, w_ref, h_ref, g_ref, o_ref):
    o_ref[...] = _rmsnorm_rows(h_ref[...] + _dot(a_ref[...], w_ref[...]), g_ref[...])


def _ffn_down(act, w_down, h1, norm_g, tm=512):
    m, kdim = act.shape
    d = w_down.shape[1]
    return pl.pallas_call(
        _ffn_down_kernel,
        grid=(m // tm,),
        in_specs=[
            pl.BlockSpec((tm, kdim), lambda i: (i, 0)),
            pl.BlockSpec((kdim, d), lambda i: (0, 0), pipeline_mode=pl.Buffered(1)),
            pl.BlockSpec((tm, d), lambda i: (i, 0)),
            pl.BlockSpec((1, d), lambda i: (0, 0)),
        ],
        out_specs=pl.BlockSpec((tm, d), lambda i: (i, 0)),
        out_shape=jax.ShapeDtypeStruct((m, d), F32),
        compiler_params=_params("parallel"),
        name="ffn_down_norm",
    )(act, w_down, h1, norm_g.reshape(1, d))


def kernel(x, norm1_g, w_in, hg_lb_param, hg_norm_g, w_proj_a, w_proj_b, w_out,
           norm2_g, w_up, conv_w, conv_b, w_down, norm_f_g):
    batch, seq, d = x.shape
    depth = w_in.shape[0]
    assert (seq, d, depth) == (SEQ, D_MODEL, 1) and w_in.shape[2] == IN_COLS
    h = x.reshape(batch * seq, d)
    for l in range(depth):
        u = _norm_bf16(h, norm1_g[l])
        proj, (wpa, wpb, wo, wdn) = _in_proj(u, w_in[l], (w_proj_a[l], w_proj_b[l], w_out[l], w_down[l]))
        oa = _hgrn2(proj, hg_lb_param, hg_norm_g[l], batch)
        ob = _attention([_attn_proj(u, w_in[l], g) for g in range(len(ATTN_GROUPS))], batch)
        h, v = _mix(oa, ob, proj, h, wpa, wpb, wo, norm2_g[l])
        act = _ffn_up(v, w_up[l], conv_w[l], conv_b[l])
        out = _ffn_down(act, wdn, h, norm_f_g)
    return out.reshape(batch, seq, d)
```

```python
import functools
import math

import numpy as np
import jax
import jax.numpy as jnp
from jax import lax
from jax.experimental import pallas as pl
from jax.experimental.pallas import tpu as pltpu

D_MODEL = 2048
SEQ = 2048
HG_HEADS = 16
HG_DIM = 128
HG_WIDTH = HG_HEADS * HG_DIM
ATTN_GROUPS = ((128, 1), (512, 4), (2048, 16))
ATTN_HEADS_PER_GROUP = 4
HEAD_DIM = 128
ATTN_WIDTH = len(ATTN_GROUPS) * ATTN_HEADS_PER_GROUP * HEAD_DIM
ATTN_OUT = ATTN_HEADS_PER_GROUP * HEAD_DIM
ROPE_THETA = 10000.0
D_FF = 5632
NORM_EPS = 1e-6
NEG_INF = -1e30
IN_COLS = 4 * HG_WIDTH + 3 * ATTN_WIDTH + 2 * D_MODEL

_OFF_HQ = 0
_OFF_HF = HG_WIDTH // 128
_OFF_HI = 2 * HG_WIDTH // 128
_OFF_HG = 3 * HG_WIDTH // 128
_OFF_AQ = 4 * HG_WIDTH // 128
_OFF_AK = _OFF_AQ + ATTN_WIDTH // 128
_OFF_AV = _OFF_AK + ATTN_WIDTH // 128
_OFF_GA = 4 * HG_WIDTH // 128
_OFF_GB = _OFF_GA + D_MODEL // 128

LANES = 128
BF16_SUBLANES = 16
VMEM_LIMIT = 56 * 1024 * 1024
LOG2E = 1.4426950408889634

F32 = jnp.float32
BF16 = jnp.bfloat16


def _dot(a, b):
    return jnp.dot(a, b, preferred_element_type=F32)


def _dot_nt(a, b):
    return lax.dot_general(a, b, (((1,), (1,)), ((), ())), preferred_element_type=F32)


def _dot_tn(a, b):
    return lax.dot_general(a, b, (((0,), (0,)), ((), ())), preferred_element_type=F32)


def _rmsnorm_rows(x, g):
    ms = jnp.mean(x * x, axis=-1, keepdims=True)
    return x * lax.rsqrt(ms + NORM_EPS) * g


def _sigmoid(x):
    return 1.0 / (1.0 + jnp.exp2(x * (-LOG2E)))


def _params(*sem):
    return pltpu.CompilerParams(dimension_semantics=sem, vmem_limit_bytes=VMEM_LIMIT)


def _norm_kernel(x_ref, g_ref, u_ref):
    u_ref[...] = _rmsnorm_rows(x_ref[...], g_ref[...]).astype(u_ref.dtype)


def _norm_bf16(x2, norm_g, tm=2048):
    m, d = x2.shape
    return pl.pallas_call(
        _norm_kernel,
        grid=(m // tm,),
        in_specs=[pl.BlockSpec((tm, d), lambda i: (i, 0)), pl.BlockSpec((1, d), lambda i: (0, 0))],
        out_specs=pl.BlockSpec((tm, d), lambda i: (i, 0)),
        out_shape=jax.ShapeDtypeStruct((m, d), BF16),
        compiler_params=_params("parallel"),
        name="norm1",
    )(x2, norm_g.reshape(1, d))


def _inproj_kernel(u_ref, w_ref, *refs):
    n_side = (len(refs) - 1) // 2
    o_ref = refs[n_side]
    o_ref[...] = _dot(u_ref[...], w_ref[...].astype(BF16))
    for src, dst in zip(refs[:n_side], refs[n_side + 1:]):
        dst[...] = src[...].astype(dst.dtype)


def _in_proj(u, w, side_weights, tm=4096, tn=512):
    m, d = u.shape
    n_hg, skip = 4 * HG_WIDTH // tn, 3 * ATTN_WIDTH // tn
    nj = n_hg + 2 * D_MODEL // tn
    steps = (m // tm) * nj
    side_in, side_out, side_shape = [], [], []
    for sw in side_weights:
        rows = -(-sw.shape[0] // steps)
        rows = -(-rows // BF16_SUBLANES) * BF16_SUBLANES
        while sw.shape[0] % rows:
            rows += BF16_SUBLANES
        last = sw.shape[0] // rows - 1
        spec = pl.BlockSpec((rows, sw.shape[1]), lambda i, j, last=last: (jnp.minimum(i * nj + j, last), 0))
        side_in.append(spec)
        side_out.append(spec)
        side_shape.append(jax.ShapeDtypeStruct(sw.shape, BF16))
    outs = pl.pallas_call(
        _inproj_kernel,
        grid=(m // tm, nj),
        in_specs=[
            pl.BlockSpec((tm, d), lambda i, j: (i, 0), pipeline_mode=pl.Buffered(1)),
            pl.BlockSpec((d, tn), lambda i, j: (0, jnp.where(j < n_hg, j, j + skip))),
            *side_in,
        ],
        out_specs=[pl.BlockSpec((tm, tn), lambda i, j: (i, j)), *side_out],
        out_shape=[jax.ShapeDtypeStruct((m, nj * tn), F32), *side_shape],
        compiler_params=_params("arbitrary", "arbitrary"),
        name="in_proj",
    )(u, w, *side_weights)
    return outs[0], outs[1:]


def _attn_proj_kernel(u_ref, w_ref, cos_ref, sin_ref, o_ref, scr, scr2, *, dil, sub):
    w = w_ref[...].astype(BF16)
    tm = u_ref.shape[0]
    length, n = tm // dil, sub // dil
    for r0 in range(0, tm, sub):
        acc = _dot(u_ref[r0:r0 + sub, :], w)
        cs, sn = cos_ref[r0:r0 + sub, :], sin_ref[r0:r0 + sub, :]
        for hh in range(acc.shape[1] // HEAD_DIM):
            ln = slice(hh * HEAD_DIM, (hh + 1) * HEAD_DIM)
            x = acc[:, ln]
            y = x * cs + pltpu.roll(x, HEAD_DIM // 2, axis=1) * sn
            if dil == 1:
                o_ref[r0:r0 + sub, ln] = y.astype(o_ref.dtype)
                continue
            scr[hh] = y
            for r1 in range(4):
                part = scr[hh, pl.ds(r1, sub // 4, stride=4), :]
                if dil == 4:
                    dst = r1 * length + r0 // dil
                    o_ref[dst:dst + n, ln] = part.astype(o_ref.dtype)
                    continue
                scr2[hh, r1] = part
                for r2 in range(4):
                    dst = _attn_slot(r1 + 4 * r2) * length + r0 // dil
                    o_ref[dst:dst + n, ln] = scr2[hh, r1, pl.ds(r2, n, stride=4), :].astype(o_ref.dtype)


def _attn_slot(r):
    return (r % 4) * 4 + r // 4


def _attn_proj(u, w, group, tm=SEQ, sub=512):
    m, d = u.shape
    dil = ATTN_GROUPS[group][1]
    assert dil in (1, 4, 16)
    tn = ATTN_HEADS_PER_GROUP * HEAD_DIM
    half = HEAD_DIM // 2
    inv_freq = jnp.exp(-math.log(ROPE_THETA) * jnp.arange(half, dtype=F32) * (2.0 / HEAD_DIM))
    ang = jnp.arange(SEQ, dtype=F32)[:, None] * inv_freq[None, :]
    cos, sin = jnp.cos(ang), jnp.sin(ang)
    one, zero = jnp.ones((SEQ, HEAD_DIM), F32), jnp.zeros((SEQ, HEAD_DIM), F32)
    cos_t = jnp.stack([jnp.concatenate([cos, cos], axis=1), one])
    sin_t = jnp.stack([jnp.concatenate([-sin, sin], axis=1), zero])
    first = (_OFF_AQ * LANES) // tn + group
    per = ATTN_WIDTH // tn
    table = pl.BlockSpec((None, SEQ, HEAD_DIM), lambda i, j: (j // 2, 0, 0))
    return pl.pallas_call(
        functools.partial(_attn_proj_kernel, dil=dil, sub=sub),
        grid=(m // tm, 3),
        in_specs=[
            pl.BlockSpec((tm, d), lambda i, j: (i, 0)),
            pl.BlockSpec((d, tn), lambda i, j: (0, first + per * j)),
            table, table,
        ],
        out_specs=pl.BlockSpec((tm, tn), lambda i, j: (i, j)),
        out_shape=jax.ShapeDtypeStruct((m, 3 * tn), BF16),
        scratch_shapes=[pltpu.VMEM((tn // HEAD_DIM, sub, HEAD_DIM), F32),
                        pltpu.VMEM((tn // HEAD_DIM, 4, sub // 4, HEAD_DIM), F32)],
        compiler_params=_params("parallel", "parallel"),
        name="attn_proj",
    )(u, w, cos_t, sin_t)


HG_CHUNK = 128
HG_HEADS_PER_STEP = 2
HG_CHUNKS_PER_STEP = 16


def _hgrn_tables(c):
    t = np.arange(c)[:, None]
    s = np.arange(c)[None, :]
    x = t ^ s
    lvl = np.where(x > 0, np.floor(np.log2(np.maximum(x, 1))), -1).astype(np.int32)
    lvl = np.where(s > t, -2, lvl).astype(np.int32)
    tri = (s <= t).astype(np.float32)
    return jnp.asarray(tri, BF16), jnp.asarray(lvl, jnp.int32)


def _neg_abs(d):
    bits = lax.bitcast_convert_type(d, jnp.uint32) | jnp.uint32(0x80000000)
    return lax.bitcast_convert_type(bits, F32)


def _hgrn_chunk(zq, zf, vv, zg, sts, lb, oml, ng, tri, lvl, row):
    heads = range(len(zq))
    c = zq[0].shape[0]
    f = [lb[i] + oml[i] * _sigmoid(zf[i]) for i in heads]
    kk = [1.0 - f[i] for i in heads]
    qq = [zq[i] * _sigmoid(zq[i]) for i in heads]
    gl = [jnp.log(f[i]) * LOG2E for i in heads]

    gcum = []
    for i in heads:
        g1 = gl[i].astype(BF16)
        g2 = (gl[i] - g1.astype(F32)).astype(BF16)
        gc = _dot(tri, jnp.concatenate([g1, g2], axis=1))
        gcum.append(gc[:, :LANES] + gc[:, LANES:])
    glast = [gcum[i][c - 1:c, :] for i in heads]

    qb = [qq[i].astype(BF16) for i in heads]
    kb = [kk[i].astype(BF16) for i in heads]
    st_in, a = list(sts), []
    for i in heads:
        kd = (kk[i] * jnp.exp2(glast[i] - gcum[i])).astype(BF16)
        st_in.append(st_in[i] * jnp.exp2(glast[i]) + _dot_tn(vv[i], kd))
        a.append(jnp.where(lvl == -1, _dot_nt(qb[i], kb[i]), 0.0))

    h = c // 2
    while h >= 1:
        in_level = lvl == int(math.log2(h))
        for i in heads:
            if h >= 4:
                g3d = gcum[i].reshape(c // (2 * h), 2 * h, LANES)
                ref = jnp.broadcast_to(g3d[:, h - 1:h, :], g3d.shape).reshape(c, LANES)
                x = _neg_abs(gcum[i] - ref)
            elif h == 2:
                pos = row & 3
                up = pltpu.roll(gl[i], c - 1, axis=0)
                dn = pltpu.roll(gl[i], 1, axis=0)
                x = jnp.where(pos == 0, up, jnp.where(pos == 1, 0.0, jnp.where(pos == 2, gl[i], gl[i] + dn)))
            else:
                x = jnp.where((row & 1) == 1, gl[i], 0.0)
            e = jnp.exp2(x).astype(BF16)
            a[i] = jnp.where(in_level, _dot_nt(qb[i] * e, kb[i] * e), a[i])
        h //= 2
    o = []
    for i in heads:
        qg = (qq[i] * jnp.exp2(gcum[i])).astype(BF16)
        o.append(_dot_nt(qg, st_in[i].astype(BF16)) + _dot(a[i].astype(BF16), vv[i]))
    y = [_rmsnorm_rows(o[i], ng) * (zg[i] * _sigmoid(zg[i])) for i in heads]
    return y, st_in[len(zq):]


def _hgrn_kernel(q_ref, f_ref, i_ref, g_ref, lbp_ref, ng_ref, tri_ref, lvl_ref, o_ref, *, chunk, heads):
    c = chunk
    seq = q_ref.shape[0]
    p = lbp_ref[...]
    pe = jnp.exp(p - jnp.max(p, axis=0, keepdims=True))
    lb = pe[0:1, :] / jnp.sum(pe, axis=0, keepdims=True)
    oml = 1.0 - lb
    ng = ng_ref[...]
    row = lax.broadcasted_iota(jnp.int32, (c, LANES), 0)
    lanes = [slice(hh * LANES, (hh + 1) * LANES) for hh in range(heads)]
    nch = HG_CHUNKS_PER_STEP

    def body(ci, sts):
        ent = [(pl.ds(pl.multiple_of((ci * nch + ch) * c, c), c), ln) for ch in range(nch) for ln in lanes]
        y, new = _hgrn_chunk(
            [q_ref[rows, ln] for rows, ln in ent], [f_ref[rows, ln] for rows, ln in ent],
            [i_ref[rows, ln].astype(BF16) for rows, ln in ent], [g_ref[rows, ln] for rows, ln in ent],
            sts, [lb[:, ln] for _, ln in ent], [oml[:, ln] for _, ln in ent],
            ng, tri_ref[...], lvl_ref[...], row)
        for (rows, ln), yy in zip(ent, y):
            o_ref[rows, ln] = yy.astype(o_ref.dtype)
        return tuple(new)

    lax.fori_loop(0, seq // (c * nch), body, tuple(jnp.zeros((HG_DIM, HG_DIM), F32) for _ in range(heads)))


def _hgrn2(proj, lb_param, norm_g, batch):
    c, hp = HG_CHUNK, HG_HEADS_PER_STEP
    tri, lvl = _hgrn_tables(c)
    m = proj.shape[0]
    w = hp * LANES

    def col(off):
        return pl.BlockSpec((SEQ, w), lambda b, h, o=off // hp: (b, o + h))

    return pl.pallas_call(
        functools.partial(_hgrn_kernel, chunk=c, heads=hp),
        grid=(batch, HG_HEADS // hp),
        in_specs=[
            col(_OFF_HQ), col(_OFF_HF), col(_OFF_HI), col(_OFF_HG),
            pl.BlockSpec((lb_param.shape[0], w), lambda b, h: (0, h)),
            pl.BlockSpec((1, LANES), lambda b, h: (0, 0)),
            pl.BlockSpec((c, c), lambda b, h: (0, 0)),
            pl.BlockSpec((c, c), lambda b, h: (0, 0)),
        ],
        out_specs=pl.BlockSpec((SEQ, w), lambda b, h: (b, h)),
        out_shape=jax.ShapeDtypeStruct((m, HG_WIDTH), BF16),
        compiler_params=_params("parallel", "parallel"),
        name="hgrn2",
    )(proj, proj, proj, proj, lb_param, norm_g.reshape(1, LANES), tri, lvl)


ATTN_BLOCK = 128
ATTN_BATCH = 8


def _attn_kernel(q0, k0, v0, q1, k1, v1, q2, k2, v2, o_ref, og_ref, lse_ref):
    blk = ATTN_BLOCK
    scale = HEAD_DIM ** -0.5
    qi = lax.broadcasted_iota(jnp.int32, (blk, 2 * blk), 0)
    ki = lax.broadcasted_iota(jnp.int32, (blk, 2 * blk), 1)
    mask2 = (ki >= qi) & (ki <= qi + blk)
    mask1 = (lax.broadcasted_iota(jnp.int32, (blk, blk), 1)
             <= lax.broadcasted_iota(jnp.int32, (blk, blk), 0))

    def run_batch(items):
        s = [_dot_nt(qj, kw) * (scale * LOG2E) for qj, kw, _, _, _ in items]
        ps, mxs = [], []
        for (_, _, _, mask, _), sj in zip(items, s):
            sj = jnp.where(mask, sj, NEG_INF)
            mx = jnp.max(sj, axis=-1, keepdims=True)
            ps.append(jnp.exp2(sj - mx).astype(BF16))
            mxs.append(mx)
        os = [_dot(p, jnp.concatenate([vw, jnp.ones_like(vw)], axis=1)) for p, (_, _, vw, _, _) in zip(ps, items)]
        for (_, _, _, _, (g, rows)), o, mx in zip(items, os, mxs):
            den = o[:, HEAD_DIM:]
            og_ref.at[g][rows, :] = o[:, :HEAD_DIM] * (1.0 / den)
            lse_ref.at[g][rows, :] = mx + jnp.log(den) * LOG2E

    refs = ((q0, k0, v0), (q1, k1, v1), (q2, k2, v2))
    pending = []
    for g, (window, dil) in enumerate(ATTN_GROUPS):
        assert window // dil == blk
        q_ref, k_ref, v_ref = refs[g]
        length = SEQ // dil
        for r in range(dil):
            base = (_attn_slot(r) if dil == 16 else r) * length
            for j in range(length // blk):
                lo, hi = base + max(j - 1, 0) * blk, base + (j + 1) * blk
                rows = pl.ds(j * blk, blk) if dil == 1 else pl.ds(j * blk * dil + r, blk, stride=dil)
                pending.append((q_ref[hi - blk:hi, :], k_ref[lo:hi, :], v_ref[lo:hi, :],
                                mask1 if j == 0 else mask2, (g, rows)))
                if len(pending) == ATTN_BATCH:
                    run_batch(pending)
                    pending = []
    assert not pending

    l0, l1, l2 = lse_ref[0], lse_ref[1], lse_ref[2]
    mx = jnp.maximum(jnp.maximum(l0, l1), l2)
    w0, w1, w2 = jnp.exp2(l0 - mx), jnp.exp2(l1 - mx), jnp.exp2(l2 - mx)
    o = (w0 * og_ref[0] + w1 * og_ref[1] + w2 * og_ref[2]) / (w0 + w1 + w2)
    o_ref[...] = o.astype(o_ref.dtype)


def _attention(qkv, batch):
    m = qkv[0].shape[0]
    hg = ATTN_HEADS_PER_GROUP

    def col(part):
        return pl.BlockSpec((SEQ, LANES), lambda b, h, o=part * hg: (b, o + h))

    return pl.pallas_call(
        _attn_kernel,
        grid=(batch, hg),
        in_specs=[col(0), col(1), col(2)] * len(ATTN_GROUPS),
        out_specs=pl.BlockSpec((SEQ, LANES), lambda b, h: (b, h)),
        out_shape=jax.ShapeDtypeStruct((m, ATTN_OUT), BF16),
        scratch_shapes=[pltpu.VMEM((len(ATTN_GROUPS), SEQ, LANES), F32)] * 2,
        compiler_params=_params("parallel", "parallel"),
        name="dilated_attn",
    )(*[a for g in qkv for a in (g, g, g)])


MIX_GATE_COLS = 1024


def _mix_kernel(*refs):
    nc = D_MODEL // MIX_GATE_COLS
    oa_ref, ob_ref = refs[0], refs[1]
    ga_refs, gb_refs = refs[2:2 + nc], refs[2 + nc:2 + 2 * nc]
    x_ref, wpa_ref, wpb_ref, wo_ref, n2_ref, h_ref, v_ref = refs[2 + 2 * nc:]
    oa, ob = oa_ref[...], ob_ref[...]
    parts = []
    for c in range(nc):
        cols = slice(c * MIX_GATE_COLS, (c + 1) * MIX_GATE_COLS)
        ya = _dot(oa, wpa_ref[:, cols])
        yb = _dot(ob, wpb_ref[:, cols])
        mix = jax.nn.sigmoid(ga_refs[c][...]) * ya + jax.nn.sigmoid(gb_refs[c][...]) * yb
        parts.append(mix.astype(BF16))
    h = x_ref[...] + _dot(jnp.concatenate(parts, axis=1), wo_ref[...])
    h_ref[...] = h
    v_ref[...] = _rmsnorm_rows(h, n2_ref[...]).astype(v_ref.dtype)


def _mix(oa, ob, proj, x2, wpa, wpb, wo, norm2_g, tm=256):
    m, d = x2.shape
    gc = MIX_GATE_COLS
    nc = d // gc
    resident = dict(pipeline_mode=pl.Buffered(1))

    def gate(off, c):
        return pl.BlockSpec((tm, gc), lambda i, o=off * LANES // gc + c: (i, o))

    return pl.pallas_call(
        _mix_kernel,
        grid=(m // tm,),
        in_specs=[
            pl.BlockSpec((tm, HG_WIDTH), lambda i: (i, 0)),
            pl.BlockSpec((tm, ATTN_OUT), lambda i: (i, 0)),
            *[gate(_OFF_GA, c) for c in range(nc)],
            *[gate(_OFF_GB, c) for c in range(nc)],
            pl.BlockSpec((tm, d), lambda i: (i, 0)),
            pl.BlockSpec(wpa.shape, lambda i: (0, 0), **resident),
            pl.BlockSpec(wpb.shape, lambda i: (0, 0), **resident),
            pl.BlockSpec(wo.shape, lambda i: (0, 0), **resident),
            pl.BlockSpec((1, d), lambda i: (0, 0)),
        ],
        out_specs=[pl.BlockSpec((tm, d), lambda i: (i, 0)), pl.BlockSpec((tm, d), lambda i: (i, 0))],
        out_shape=[jax.ShapeDtypeStruct((m, d), F32), jax.ShapeDtypeStruct((m, d), BF16)],
        compiler_params=_params("parallel"),
        name="mix_outproj",
    )(oa, ob, *([proj] * (2 * nc)), x2, wpa, wpb, wo, norm2_g.reshape(1, d))


def _shift_rows(h, prev, k):
    r = pltpu.roll(h, k, axis=0)
    p = pltpu.roll(prev, k, axis=0)
    row8 = lax.broadcasted_iota(jnp.int32, prev.shape, 0)
    top = jnp.where(row8 < k, p, r[:8])
    return jnp.concatenate([top, r[8:]], axis=0)


def _causal_conv3(h, prev, w, b):
    return w[2:3] * h + w[1:2] * _shift_rows(h, prev, 1) + w[0:1] * _shift_rows(h, prev, 2) + b


def _gelu_tanh_gate(x, y):
    c = math.sqrt(2.0 / math.pi)
    w = x * ((-2.0 * c * LOG2E) + (-2.0 * c * 0.044715 * LOG2E) * (x * x))
    return (x * y) * (1.0 / (1.0 + jnp.exp2(w)))


def _ffn_up_kernel(v_ref, halo_ref, wa_ref, wb_ref, cwa_ref, cwb_ref, cba_ref, cbb_ref, o_ref, *, sub):
    tm = v_ref.shape[0]
    hr = halo_ref.shape[0]
    starts_sequence = (pl.program_id(0) * tm) % SEQ == 0
    halo = jnp.where(starts_sequence, jnp.zeros_like(halo_ref[...]), halo_ref[...])
    wa, wb = wa_ref[...].astype(BF16), wb_ref[...].astype(BF16)
    cwa, cwb, cba, cbb = cwa_ref[...], cwb_ref[...], cba_ref[...], cbb_ref[...]
    pa = pb = None
    for r0 in range(0, tm, sub):
        vs = v_ref[r0:r0 + sub, :]
        if r0 == 0:
            vs = jnp.concatenate([halo, vs], axis=0)
        ha, hb = _dot(vs, wa), _dot(vs, wb)
        if r0 == 0:
            pa, pb = ha[hr - 8:hr], hb[hr - 8:hr]
            ha, hb = ha[hr:], hb[hr:]
        ca = _causal_conv3(ha, pa, cwa, cba)
        cb = _causal_conv3(hb, pb, cwb, cbb)
        o_ref[r0:r0 + sub, :] = _gelu_tanh_gate(ca, cb).astype(o_ref.dtype)
        pa, pb = ha[sub - 8:], hb[sub - 8:]


def _ffn_up(v, w_up, conv_w, conv_b, tm=2048, tn=512, sub=512):
    m, d = v.shape
    nj = D_FF // tn
    hb = tm // BF16_SUBLANES
    return pl.pallas_call(
        functools.partial(_ffn_up_kernel, sub=sub),
        grid=(m // tm, nj),
        in_specs=[
            pl.BlockSpec((tm, d), lambda i, j: (i, 0)),
            pl.BlockSpec((BF16_SUBLANES, d), lambda i, j: (jnp.maximum(i * hb - 1, 0), 0)),
            pl.BlockSpec((d, tn), lambda i, j: (0, j)),
            pl.BlockSpec((d, tn), lambda i, j: (0, j + nj)),
            pl.BlockSpec((3, tn), lambda i, j: (0, j)),
            pl.BlockSpec((3, tn), lambda i, j: (0, j + nj)),
            pl.BlockSpec((1, tn), lambda i, j: (0, j)),
            pl.BlockSpec((1, tn), lambda i, j: (0, j + nj)),
        ],
        out_specs=pl.BlockSpec((tm, tn), lambda i, j: (i, j)),
        out_shape=jax.ShapeDtypeStruct((m, D_FF), BF16),
        compiler_params=_params("parallel", "parallel"),
        name="ffn_up_conv_gate",
    )(v, v, w_up, w_up, conv_w, conv_w, conv_b.reshape(1, -1), conv_b.reshape(1, -1))


def _ffn_down_kernel(a_ref, w_ref, h_ref, g_ref, o_ref):
    o_ref[...] = _rmsnorm_rows(h_ref[...] + _dot(a_ref[...], w_ref[...]), g_ref[...])


def _ffn_down(act, w_down, h1, norm_g, tm=512):
    m, kdim = act.shape
    d = w_down.shape[1]
    return pl.pallas_call(
        _ffn_down_kernel,
        grid=(m // tm,),
        in_specs=[
            pl.BlockSpec((tm, kdim), lambda i: (i, 0)),
            pl.BlockSpec((kdim, d), lambda i: (0, 0), pipeline_mode=pl.Buffered(1)),
            pl.BlockSpec((tm, d), lambda i: (i, 0)),
            pl.BlockSpec((1, d), lambda i: (0, 0)),
        ],
        out_specs=pl.BlockSpec((tm, d), lambda i: (i, 0)),
        out_shape=jax.ShapeDtypeStruct((m, d), F32),
        compiler_params=_params("parallel"),
        name="ffn_down_norm",
    )(act, w_down, h1, norm_g.reshape(1, d))


def kernel(x, norm1_g, w_in, hg_lb_param, hg_norm_g, w_proj_a, w_proj_b, w_out,
           norm2_g, w_up, conv_w, conv_b, w_down, norm_f_g):
    batch, seq, d = x.shape
    depth = w_in.shape[0]
    assert (seq, d, depth) == (SEQ, D_MODEL, 1) and w_in.shape[2] == IN_COLS
    h = x.reshape(batch * seq, d)
    for l in range(depth):
        u = _norm_bf16(h, norm1_g[l])
        proj, (wpa, wpb, wo, wdn) = _in_proj(u, w_in[l], (w_proj_a[l], w_proj_b[l], w_out[l], w_down[l]))
        oa = _hgrn2(proj, hg_lb_param, hg_norm_g[l], batch)
        ob = _attention([_attn_proj(u, w_in[l], g) for g in range(len(ATTN_GROUPS))], batch)
        h, v = _mix(oa, ob, proj, h, wpa, wpb, wo, norm2_g[l])
        act = _ffn_up(v, w_up[l], conv_w[l], conv_b[l])
        out = _ffn_down(act, wdn, h, norm_f_g)
    return out.reshape(batch, seq, d)
```
